```python
import jax, jax.numpy as jnp
from jax import lax
import numpy as np

D_MODEL = 4096
BATCH = 1
SEQ = 8192
DEPTH = 1
DEC_BATCH = 16
DEC_SEQ = 32
PAST_LEN = 2048

CHUNK = 64
POOL_WIDTH = D_MODEL // 2
POOL_WINDOWS = (2, 4, 8, 16)
POOL_GROUP = POOL_WIDTH // len(POOL_WINDOWS)
POOL_HIST = max(POOL_WINDOWS) - 1
ATTN_WIDTH = D_MODEL - POOL_WIDTH
HEAD_DIM = 128
N_HEADS = ATTN_WIDTH // HEAD_DIM
PROJ_WIDTH = POOL_WIDTH + 3 * ATTN_WIDTH + N_HEADS
D_FF = 4 * D_MODEL
Q_BLOCK = 128
NORM_EPS = 1e-6
ATTN_SCALE = HEAD_DIM ** -0.5
MASK_VALUE = -1e30
FORGET_BIAS_INIT = 2.0

kernel_name = 'hymba_pool_fox_stream_step'


def rms_norm(x, g):
    xf = x.astype(jnp.float32)
    y = xf * lax.rsqrt(jnp.mean(xf * xf, axis=-1, keepdims=True) + NORM_EPS)
    return (y * g.astype(jnp.float32)).astype(x.dtype)


def pool_mixer(u, hist, pos, w_pool, pool_scale):
    B, S, _ = u.shape
    ext = jnp.concatenate([hist.astype(u.dtype), u], axis=1)
    extf = ext.astype(jnp.float32)
    csum = jnp.concatenate([jnp.zeros((B, 1, POOL_WIDTH), jnp.float32),
                            lax.cumsum(extf, axis=1)], axis=1)
    end = csum[:, POOL_HIST + 1:]
    cur = extf[:, POOL_HIST:]
    outs = []
    for g, w in enumerate(POOL_WINDOWS):
        lo, hi = g * POOL_GROUP, (g + 1) * POOL_GROUP
        start = csum[:, POOL_HIST + 1 - w:POOL_HIST + 1 - w + S, lo:hi]
        cnt = jnp.minimum(pos + 1, w).astype(jnp.float32)[None, :, None]
        d = (end[..., lo:hi] - start) / cnt - cur[..., lo:hi]
        outs.append(jnp.einsum('bsc,cd->bsd', d, w_pool[g].astype(jnp.float32)))
    out = jnp.concatenate(outs, axis=-1) * pool_scale.astype(jnp.float32)
    return out.astype(u.dtype), ext[:, -POOL_HIST:]


def fox_block(q, k, v, cq, ck, qpos, kpos):
    s = jnp.einsum('bqhd,bkhd->bhqk', q, k).astype(jnp.float32) * ATTN_SCALE
    s = s + (jnp.swapaxes(cq, 1, 2)[..., :, None] - jnp.swapaxes(ck, 1, 2)[..., None, :])
    s = jnp.where(kpos[None, :] <= qpos[:, None], s, MASK_VALUE)
    p = jax.nn.softmax(s, axis=-1)
    return jnp.einsum('bhqk,bkhd->bqhd', p.astype(v.dtype), v)


def layer(x, pool_hist, past_k, past_v, past_logf, attn_norm_g, w_in, b_f, q_norm_g, k_norm_g,
          w_pool, pool_scale, w_out, mlp_norm_g, w_up, w_down):
    B, S, _ = x.shape
    P = past_k.shape[1]
    pos = jnp.arange(P, P + S)
    kpos = jnp.arange(P + S)
    h = rms_norm(x, attn_norm_g)
    proj = jnp.einsum('bsd,de->bse', h, w_in)
    a0 = POOL_WIDTH
    u = proj[..., :a0]
    q = proj[..., a0:a0 + ATTN_WIDTH].reshape(B, S, N_HEADS, HEAD_DIM)
    k = proj[..., a0 + ATTN_WIDTH:a0 + 2 * ATTN_WIDTH].reshape(B, S, N_HEADS, HEAD_DIM)
    v = proj[..., a0 + 2 * ATTN_WIDTH:a0 + 3 * ATTN_WIDTH].reshape(B, S, N_HEADS, HEAD_DIM)
    f_logit = proj[..., a0 + 3 * ATTN_WIDTH:]
    q = rms_norm(q, q_norm_g)
    k = rms_norm(k, k_norm_g)
    logf = jax.nn.log_sigmoid(f_logit.astype(jnp.float32) + b_f.astype(jnp.float32))
    pool_out, new_hist = pool_mixer(u, pool_hist, pos, w_pool, pool_scale)
    k_all = jnp.concatenate([past_k.astype(k.dtype), k], axis=1)
    v_all = jnp.concatenate([past_v.astype(v.dtype), v], axis=1)
    c_all = lax.cumsum(jnp.concatenate([past_logf.astype(jnp.float32), logf], axis=1), axis=1)
    cq = c_all[:, P:]
    if S <= Q_BLOCK:
        attn = fox_block(q, k_all, v_all, cq, c_all, pos, kpos)
    else:
        nb = S // Q_BLOCK
        qb = jnp.swapaxes(q.reshape(B, nb, Q_BLOCK, N_HEADS, HEAD_DIM), 0, 1)
        cqb = jnp.swapaxes(cq.reshape(B, nb, Q_BLOCK, N_HEADS), 0, 1)
        qposb = pos.reshape(nb, Q_BLOCK)
        outb = lax.map(lambda a: fox_block(a[0], k_all, v_all, a[1], c_all, a[2], kpos),
                       (qb, cqb, qposb))
        attn = jnp.swapaxes(outb, 0, 1).reshape(B, S, N_HEADS, HEAD_DIM)
    mixed = jnp.concatenate([pool_out, attn.reshape(B, S, ATTN_WIDTH).astype(pool_out.dtype)], axis=-1)
    x = x + jnp.einsum('bse,ed->bsd', mixed, w_out)
    h2 = rms_norm(x, mlp_norm_g)
    hid = jnp.square(jax.nn.relu(jnp.einsum('bsd,df->bsf', h2, w_up)))
    x = x + jnp.einsum('bsf,fd->bsd', hid, w_down)
    return x, k, v, logf, new_hist


def setup_inputs(seed: int = 0) -> dict:
    key = jax.random.key(seed)
    ks = jax.random.split(key, 20)
    n = jax.random.normal
    f32 = jnp.float32
    x_prompt = n(ks[0], (BATCH, SEQ, D_MODEL), f32)
    x_sample = n(ks[1], (DEC_BATCH, DEC_SEQ, D_MODEL), f32)
    cache_k = n(ks[2], (DEPTH, DEC_BATCH, PAST_LEN, N_HEADS, HEAD_DIM), f32)
    cache_v = n(ks[3], (DEPTH, DEC_BATCH, PAST_LEN, N_HEADS, HEAD_DIM), f32)
    cache_logf = jax.nn.log_sigmoid(FORGET_BIAS_INIT + n(ks[4], (DEPTH, DEC_BATCH, PAST_LEN, N_HEADS), f32))
    state_pool = n(ks[5], (DEPTH, DEC_BATCH, POOL_HIST, POOL_WIDTH), f32)
    attn_norm_g = 1.0 + 0.02 * n(ks[6], (DEPTH, D_MODEL), f32)
    w_in = n(ks[7], (DEPTH, D_MODEL, PROJ_WIDTH), f32) * D_MODEL ** -0.5
    b_f = FORGET_BIAS_INIT + 0.1 * n(ks[8], (DEPTH, N_HEADS), f32)
    q_norm_g = 1.0 + 0.02 * n(ks[9], (DEPTH, HEAD_DIM), f32)
    k_norm_g = 1.0 + 0.02 * n(ks[10], (DEPTH, HEAD_DIM), f32)
    w_pool = n(ks[11], (DEPTH, len(POOL_WINDOWS), POOL_GROUP, POOL_GROUP), f32) * POOL_GROUP ** -0.5
    pool_scale = 1.0 + 0.1 * n(ks[12], (DEPTH, POOL_WIDTH), f32)
    w_out = n(ks[13], (DEPTH, D_MODEL, D_MODEL), f32) * D_MODEL ** -0.5
    mlp_norm_g = 1.0 + 0.02 * n(ks[14], (DEPTH, D_MODEL), f32)
    w_up = n(ks[15], (DEPTH, D_MODEL, D_FF), f32) * D_MODEL ** -0.5
    w_down = n(ks[16], (DEPTH, D_FF, D_MODEL), f32) * D_FF ** -0.5
    return {'x_prompt': x_prompt, 'x_sample': x_sample, 'cache_k': cache_k, 'cache_v': cache_v,
            'cache_logf': cache_logf, 'state_pool': state_pool, 'attn_norm_g': attn_norm_g,
            'w_in': w_in, 'b_f': b_f, 'q_norm_g': q_norm_g, 'k_norm_g': k_norm_g, 'w_pool': w_pool,
            'pool_scale': pool_scale, 'w_out': w_out, 'mlp_norm_g': mlp_norm_g, 'w_up': w_up,
            'w_down': w_down}


def reference(x_prompt, x_sample, cache_k, cache_v, cache_logf, state_pool, attn_norm_g, w_in, b_f,
              q_norm_g, k_norm_g, w_pool, pool_scale, w_out, mlp_norm_g, w_up, w_down):
    yp, ys = x_prompt, x_sample
    bp = x_prompt.shape[0]
    empty_kv = jnp.zeros((bp, 0, N_HEADS, HEAD_DIM), x_prompt.dtype)
    empty_f = jnp.zeros((bp, 0, N_HEADS), jnp.float32)
    zero_hist = jnp.zeros((bp, POOL_HIST, POOL_WIDTH), x_prompt.dtype)
    kp, vp, fp, hp, ksm, vsm, fsm, hsm = [], [], [], [], [], [], [], []
    for l in range(DEPTH):
        wl = (attn_norm_g[l], w_in[l], b_f[l], q_norm_g[l], k_norm_g[l], w_pool[l], pool_scale[l],
              w_out[l], mlp_norm_g[l], w_up[l], w_down[l])
        yp, k1, v1, f1, h1 = layer(yp, zero_hist, empty_kv, empty_kv, empty_f, *wl)
        ys, k2, v2, f2, h2 = layer(ys, state_pool[l], cache_k[l], cache_v[l], cache_logf[l], *wl)
        kp.append(k1); vp.append(v1); fp.append(f1); hp.append(h1)
        ksm.append(k2); vsm.append(v2); fsm.append(f2); hsm.append(h2)
    return (yp, ys, jnp.stack(kp), jnp.stack(vp), jnp.stack(fp), jnp.stack(hp),
            jnp.stack(ksm), jnp.stack(vsm), jnp.stack(fsm), jnp.stack(hsm))
```

```python
import functools

import jax
import jax.numpy as jnp
from jax import lax
from jax.experimental import pallas as pl
from jax.experimental.pallas import tpu as pltpu

HEAD_DIM = 128
POOL_WINDOWS = (2, 4, 8, 16)
POOL_HIST = max(POOL_WINDOWS) - 1
HALO = POOL_HIST + 1
NORM_EPS = 1e-6
MASK_VALUE = -1e30
N_SPLIT = 3
MIB = 1024 * 1024

F32 = jnp.float32
BF16 = jnp.bfloat16


def _params(semantics, vmem_mib):
    return pltpu.CompilerParams(dimension_semantics=semantics, vmem_limit_bytes=vmem_mib * MIB)


def _rmsnorm_kernel(x_ref, g_ref, o_ref):
    x = x_ref[...]
    ms = jnp.mean(x * x, axis=-1, keepdims=True)
    o_ref[...] = (x * lax.rsqrt(ms + NORM_EPS) * g_ref[...]).astype(o_ref.dtype)


def _rmsnorm(x, g):
    m, d = x.shape
    tm = min(m, 256)
    return pl.pallas_call(
        _rmsnorm_kernel,
        out_shape=jax.ShapeDtypeStruct((m, d), BF16),
        grid=(m // tm,),
        in_specs=[pl.BlockSpec((tm, d), lambda i: (i, 0)), pl.BlockSpec((1, d), lambda i: (0, 0))],
        out_specs=pl.BlockSpec((tm, d), lambda i: (i, 0)),
        compiler_params=_params(("parallel",), 32),
        name="rmsnorm",
    )(x, g.reshape(1, d))


def _head_rmsnorm(a, g):
    ms = jnp.mean(a * a, axis=-1, keepdims=True)
    return a * lax.rsqrt(ms + NORM_EPS) * g


def _proj_kernel(x_ref, w_ref, *refs, mode, scale):
    acc = jnp.dot(x_ref[...], w_ref[...], preferred_element_type=F32)
    tn = acc.shape[1]
    if mode == "plain":
        (o_ref,) = refs
        o_ref[...] = acc
    elif mode == "both":
        o_ref, ob_ref = refs
        o_ref[...] = acc
        ob_ref[...] = acc.astype(BF16)
    elif mode == "relu2":
        (ob_ref,) = refs
        r = jnp.maximum(acc, 0.0)
        ob_ref[...] = (r * r).astype(BF16)
    elif mode == "qnorm":
        g_ref, ob_ref = refs
        for h in range(tn // HEAD_DIM):
            sl = slice(h * HEAD_DIM, (h + 1) * HEAD_DIM)
            ob_ref[:, sl] = (_head_rmsnorm(acc[:, sl], g_ref[...]) * scale).astype(BF16)
    elif mode == "knorm":
        g_ref, o_ref, ob_ref = refs
        for h in range(tn // HEAD_DIM):
            sl = slice(h * HEAD_DIM, (h + 1) * HEAD_DIM)
            y = _head_rmsnorm(acc[:, sl], g_ref[...])
            o_ref[:, sl] = y
            ob_ref[:, sl] = y.astype(BF16)
    else:
        raise ValueError(mode)


def _proj(x, w, col_off, ncols, mode, g=None, scale=1.0, tn=512):
    m, k = x.shape
    tm = min(m, 1024)
    tn = min(tn, ncols)
    assert m % tm == 0 and ncols % tn == 0 and col_off % tn == 0
    joff = col_off // tn
    in_specs = [pl.BlockSpec((tm, k), lambda i, j: (i, 0)),
                pl.BlockSpec((k, tn), lambda i, j: (0, j + joff))]
    args = [x, w]
    if g is not None:
        in_specs.append(pl.BlockSpec((1, HEAD_DIM), lambda i, j: (0, 0)))
        args.append(g.reshape(1, HEAD_DIM))
    ospec = pl.BlockSpec((tm, tn), lambda i, j: (i, j))
    f32_out = jax.ShapeDtypeStruct((m, ncols), F32)
    bf_out = jax.ShapeDtypeStruct((m, ncols), BF16)
    out_shape, out_specs = {
        "plain": ([f32_out], [ospec]),
        "both": ([f32_out, bf_out], [ospec, ospec]),
        "relu2": ([bf_out], [ospec]),
        "qnorm": ([bf_out], [ospec]),
        "knorm": ([f32_out, bf_out], [ospec, ospec]),
    }[mode]
    return pl.pallas_call(
        functools.partial(_proj_kernel, mode=mode, scale=scale),
        out_shape=out_shape,
        grid=(m // tm, ncols // tn),
        in_specs=in_specs,
        out_specs=out_specs,
        compiler_params=_params(("parallel", "arbitrary"), 48),
        name="proj_" + mode,
    )(*args)


def _forget_kernel(x_ref, w_ref, b_ref, o_ref):
    nh = o_ref.shape[1]
    z = jnp.dot(x_ref[...], w_ref[...], preferred_element_type=F32)[:, :nh] + b_ref[...]
    o_ref[...] = jnp.minimum(z, 0.0) - jnp.log(1.0 + jnp.exp(-jnp.abs(z)))


def _forget_proj(x, w_f, b_f):
    m, k = x.shape
    nh = b_f.shape[0]
    tm = min(m, 1024)
    return pl.pallas_call(
        _forget_kernel,
        out_shape=jax.ShapeDtypeStruct((m, nh), F32),
        grid=(m // tm,),
        in_specs=[pl.BlockSpec((tm, k), lambda i: (i, 0)),
                  pl.BlockSpec((k, HEAD_DIM), lambda i: (0, 0)),
                  pl.BlockSpec((1, nh), lambda i: (0, 0))],
        out_specs=pl.BlockSpec((tm, nh), lambda i: (i, 0)),
        compiler_params=_params(("parallel",), 32),
        name="forget_proj",
    )(x, w_f, b_f.reshape(1, nh))


def _pool_kernel(u_ref, prev_ref, hist_ref, w_ref, scale_ref, o_ref, ext_ref, *, ts, pos0, group):
    i = pl.program_id(1)

    @pl.when(i == 0)
    def _():
        ext_ref[0:HALO, :] = hist_ref[...]

    @pl.when(i > 0)
    def _():
        ext_ref[0:HALO, :] = prev_ref[...]

    ext_ref[HALO:HALO + ts, :] = u_ref[...]
    pos = lax.broadcasted_iota(jnp.int32, (ts, 1), 0) + (i * ts + pos0)
    for gi, win in enumerate(POOL_WINDOWS):
        lo = gi * group
        cur = ext_ref[HALO:HALO + ts, lo:lo + group]
        tot = cur
        for j in range(1, win):
            tot = tot + ext_ref[HALO - j:HALO - j + ts, lo:lo + group]
        cnt = jnp.minimum(pos + 1, win).astype(F32)
        d = tot / cnt - cur
        out = jnp.dot(d.astype(BF16), w_ref[gi], preferred_element_type=F32)
        o_ref[:, lo:lo + group] = (out * scale_ref[:, lo:lo + group]).astype(BF16)


def _pool_mixer(u, hist, w_pool, pool_scale, pos0):
    b, s, c = u.shape
    group = c // len(POOL_WINDOWS)
    ts = min(s, 512)
    assert s % ts == 0 and ts % HALO == 0
    hist16 = jnp.pad(hist, ((0, 0), (HALO - POOL_HIST, 0), (0, 0)))
    nprev = ts // HALO
    return pl.pallas_call(
        functools.partial(_pool_kernel, ts=ts, pos0=pos0, group=group),
        out_shape=jax.ShapeDtypeStruct((b, s, c), BF16),
        grid=(b, s // ts),
        in_specs=[pl.BlockSpec((None, ts, c), lambda bi, i: (bi, i, 0)),
                  pl.BlockSpec((None, HALO, c), lambda bi, i: (bi, jnp.maximum(i * nprev - 1, 0), 0)),
                  pl.BlockSpec((None, HALO, c), lambda bi, i: (bi, 0, 0)),
                  pl.BlockSpec(w_pool.shape, lambda bi, i: (0, 0, 0)),
                  pl.BlockSpec((1, c), lambda bi, i: (0, 0))],
        out_specs=pl.BlockSpec((None, ts, c), lambda bi, i: (bi, i, 0)),
        scratch_shapes=[pltpu.VMEM((HALO + ts, c), F32)],
        compiler_params=_params(("parallel", "arbitrary"), 40),
        name="pool_mixer",
    )(u, u, hist16, w_pool, pool_scale.reshape(1, c))


def _cumsum_kernel(x_ref, c_ref, *piece_refs):
    x = x_ref[...]
    n = x.shape[1]
    lane = lax.broadcasted_iota(jnp.int32, x.shape, 1)
    shift = 1
    while shift < n:
        x = x + jnp.where(lane >= shift, pltpu.roll(x, shift, 1), 0.0)
        shift *= 2
    c_ref[...] = x
    rem = x
    for p_ref in piece_refs:
        piece = rem.astype(BF16)
        p_ref[...] = piece
        rem = rem - piece.astype(F32)


def _cumsum_rows(x, with_pieces):
    r, n = x.shape
    npad = -n % HEAD_DIM
    xp = jnp.pad(x, ((0, 0), (0, npad)))
    shp = xp.shape
    out_shape = [jax.ShapeDtypeStruct(shp, F32)]
    if with_pieces:
        out_shape += [jax.ShapeDtypeStruct(shp, BF16)] * N_SPLIT
    spec = pl.BlockSpec(shp, lambda: (0, 0))
    outs = pl.pallas_call(
        _cumsum_kernel,
        out_shape=out_shape,
        in_specs=[spec],
        out_specs=[spec] * len(out_shape),
        compiler_params=pltpu.CompilerParams(vmem_limit_bytes=32 * MIB),
        name="forget_cumsum",
    )(xp)
    return [o[:, :n] for o in outs]


def _attn_kernel(q_ref, qa_ref, k_ref, ka_ref, v_ref, o_ref, m_ref, l_ref, acc_ref, *, tq, tk):
    qi = pl.program_id(1)
    qc = jnp.concatenate([q_ref[...], qa_ref[...]], axis=1)
    m_ref[...] = jnp.full(m_ref.shape, MASK_VALUE, F32)
    l_ref[...] = jnp.zeros(l_ref.shape, F32)
    acc_ref[...] = jnp.zeros(acc_ref.shape, F32)

    def step(off, masked):
        kc = jnp.concatenate([k_ref[pl.ds(off, tk), :], ka_ref[pl.ds(off, tk), :]], axis=1)
        s = lax.dot_general(qc, kc, (((1,), (1,)), ((), ())), preferred_element_type=F32)
        if masked:
            rows = qi * tq + lax.broadcasted_iota(jnp.int32, (tq, 1), 0)
            cols = off + lax.broadcasted_iota(jnp.int32, (1, tk), 1)
            s = jnp.where(cols <= rows, s, MASK_VALUE)
        m_prev = m_ref[...]
        m_new = jnp.maximum(m_prev, jnp.max(s, axis=1, keepdims=True))
        p = jnp.exp(s - m_new)
        alpha = jnp.exp(m_prev - m_new)
        l_ref[...] = alpha * l_ref[...] + jnp.sum(p, axis=1, keepdims=True)
        acc_ref[...] = alpha * acc_ref[...] + jnp.dot(p.astype(BF16), v_ref[pl.ds(off, tk), :],
                                                      preferred_element_type=F32)
        m_ref[...] = m_new

    per_q = tq // tk

    def body(j, carry):
        step(pl.multiple_of(j * tk, tk), masked=False)
        return carry

    lax.fori_loop(0, qi * per_q, body, 0)
    for d in range(per_q):
        step(pl.multiple_of((qi * per_q + d) * tk, tk), masked=True)
    o_ref[...] = (acc_ref[...] / l_ref[...]).astype(o_ref.dtype)


def _prompt_attention(q, k, v, qa, ka, tq=512, tk=512):
    s, width = q.shape
    nh = width // HEAD_DIM
    tq, tk = min(tq, s), min(tk, s)
    assert s % tq == 0 and tq % tk == 0
    row = lambda h, i: (i, h)
    full = lambda h, i: (0, h)
    return pl.pallas_call(
        functools.partial(_attn_kernel, tq=tq, tk=tk),
        out_shape=jax.ShapeDtypeStruct((s, width), BF16),
        grid=(nh, s // tq),
        in_specs=[pl.BlockSpec((tq, HEAD_DIM), row),
                  pl.BlockSpec((None, tq, HEAD_DIM), lambda h, i: (h, i, 0)),
                  pl.BlockSpec((s, HEAD_DIM), full),
                  pl.BlockSpec((None, s, HEAD_DIM), lambda h, i: (h, 0, 0)),
                  pl.BlockSpec((s, HEAD_DIM), full)],
        out_specs=pl.BlockSpec((tq, HEAD_DIM), row),
        scratch_shapes=[pltpu.VMEM((tq, 1), F32), pltpu.VMEM((tq, 1), F32), pltpu.VMEM((tq, HEAD_DIM), F32)],
        compiler_params=_params(("parallel", "arbitrary"), 48),
        name="prompt_attention",
    )(q, qa, k, ka, v)


def _bias_features(pieces, query_side):
    c = jnp.stack(pieces, axis=-1)
    ones = jnp.ones_like(c)
    feats = jnp.concatenate([c, ones] if query_side else [ones, -c], axis=-1)
    return jnp.pad(feats, ((0, 0), (0, 0), (0, HEAD_DIM - 2 * N_SPLIT)))


def _sample_attn_kernel(q_ref, kn_ref, vn_ref, kc_ref, vc_ref, cq_ref, cn_ref, cp_ref, o_ref, *, hg):
    sq = q_ref.shape[0]
    rows = lax.broadcasted_iota(jnp.int32, (sq, sq), 0)
    cols = lax.broadcasted_iota(jnp.int32, (sq, sq), 1)
    nt = (((1,), (1,)), ((), ()))
    for h in range(hg):
        sl = slice(h * HEAD_DIM, (h + 1) * HEAD_DIM)
        qh = q_ref[:, sl]
        cq = cq_ref[:, h:h + 1]
        s_past = lax.dot_general(qh, kc_ref[:, sl].astype(BF16), nt, preferred_element_type=F32)
        s_past = s_past + (cq - cp_ref[h:h + 1, :])
        s_new = lax.dot_general(qh, kn_ref[:, sl], nt, preferred_element_type=F32)
        s_new = jnp.where(cols <= rows, s_new + (cq - cn_ref[h:h + 1, :]), MASK_VALUE)
        m = jnp.maximum(jnp.max(s_past, axis=1, keepdims=True), jnp.max(s_new, axis=1, keepdims=True))
        p_past = jnp.exp(s_past - m)
        p_new = jnp.exp(s_new - m)
        denom = jnp.sum(p_past, axis=1, keepdims=True) + jnp.sum(p_new, axis=1, keepdims=True)
        out = jnp.dot(p_past.astype(BF16), vc_ref[:, sl].astype(BF16), preferred_element_type=F32)
        out = out + jnp.dot(p_new.astype(BF16), vn_ref[:, sl], preferred_element_type=F32)
        o_ref[:, sl] = (out / denom).astype(o_ref.dtype)


def _sample_attention(q, kn, vn, cache_k, cache_v, c_new, c_past, hg=4):
    b, sq, width = q.shape
    p = cache_k.shape[1]
    nh = width // HEAD_DIM
    hg = min(hg, nh)
    ng = nh // hg
    cq = jnp.swapaxes(c_new.reshape(b, ng, hg, sq), 2, 3)
    cn = c_new.reshape(b, ng, hg, sq)
    cp = c_past.reshape(b, ng, hg, p)
    new_spec = pl.BlockSpec((None, sq, hg * HEAD_DIM), lambda bi, g: (bi, 0, g))
    cache_spec = pl.BlockSpec((None, p, hg * HEAD_DIM), lambda bi, g: (bi, 0, g))
    return pl.pallas_call(
        functools.partial(_sample_attn_kernel, hg=hg),
        out_shape=jax.ShapeDtypeStruct((b, sq, width), BF16),
        grid=(b, ng),
        in_specs=[new_spec, new_spec, new_spec, cache_spec, cache_spec,
                  pl.BlockSpec((None, None, sq, hg), lambda bi, g: (bi, g, 0, 0)),
                  pl.BlockSpec((None, None, hg, sq), lambda bi, g: (bi, g, 0, 0)),
                  pl.BlockSpec((None, None, hg, p), lambda bi, g: (bi, g, 0, 0))],
        out_specs=new_spec,
        compiler_params=_params(("parallel", "parallel"), 48),
        name="sample_attention",
    )(q, kn, vn, cache_k, cache_v, cq, cn, cp)


def _outproj_kernel(p_ref, a_ref, wt_ref, wb_ref, r_ref, o_ref):
    acc = jnp.dot(p_ref[...], wt_ref[...], preferred_element_type=F32)
    acc = acc + jnp.dot(a_ref[...], wb_ref[...], preferred_element_type=F32)
    o_ref[...] = r_ref[...] + acc


def _out_proj(pooled, attn, w_out, resid, tn=512):
    m, half = pooled.shape
    assert attn.shape == (m, half) and w_out.shape[0] == 2 * half
    n = w_out.shape[1]
    tm = min(m, 1024)
    tn = min(tn, n)
    xspec = pl.BlockSpec((tm, half), lambda i, j: (i, 0))
    ospec = pl.BlockSpec((tm, tn), lambda i, j: (i, j))
    return pl.pallas_call(
        _outproj_kernel,
        out_shape=jax.ShapeDtypeStruct((m, n), F32),
        grid=(m // tm, n // tn),
        in_specs=[xspec, xspec,
                  pl.BlockSpec((half, tn), lambda i, j: (0, j)),
                  pl.BlockSpec((half, tn), lambda i, j: (1, j)),
                  ospec],
        out_specs=ospec,
        compiler_params=_params(("parallel", "arbitrary"), 48),
        name="out_proj",
    )(pooled, attn, w_out, w_out, resid)


def _down_kernel(x_ref, w_ref, r_ref, o_ref):
    kk = pl.program_id(2)
    d = jnp.dot(x_ref[...], w_ref[...], preferred_element_type=F32)

    @pl.when(kk == 0)
    def _():
        o_ref[...] = r_ref[...] + d

    @pl.when(kk > 0)
    def _():
        o_ref[...] += d


def _down_proj(x, w, resid, tn=512, tk=4096):
    m, f = x.shape
    n = w.shape[1]
    tm = min(m, 1024)
    tn, tk = min(tn, n), min(tk, f)
    ospec = pl.BlockSpec((tm, tn), lambda i, j, kk: (i, j))
    return pl.pallas_call(
        _down_kernel,
        out_shape=jax.ShapeDtypeStruct((m, n), F32),
        grid=(m // tm, n // tn, f // tk),
        in_specs=[pl.BlockSpec((tm, tk), lambda i, j, kk: (i, kk)),
                  pl.BlockSpec((tk, tn), lambda i, j, kk: (kk, j)),
                  ospec],
        out_specs=ospec,
        compiler_params=_params(("parallel", "arbitrary", "arbitrary"), 48),
        name="down_proj",
    )(x, w, resid)


def _layer(x, pool_hist, past, wts, pos0):
    (attn_norm_g, w_in, w_f, b_f, q_norm_g, k_norm_g, w_pool, pool_scale, w_out, mlp_norm_g, w_up, w_down) = wts
    b, s, d = x.shape
    m = b * s
    pw = pool_scale.shape[0]
    aw = d - pw
    nh = aw // HEAD_DIM
    x2 = x.reshape(m, d)

    h = _rmsnorm(x2, attn_norm_g)
    (u,) = _proj(h, w_in, 0, pw, "plain")
    (qb,) = _proj(h, w_in, pw, aw, "qnorm", g=q_norm_g, scale=HEAD_DIM ** -0.5)
    k, kb = _proj(h, w_in, pw + aw, aw, "knorm", g=k_norm_g)
    v, vb = _proj(h, w_in, pw + 2 * aw, aw, "both")
    logf = _forget_proj(h, w_f, b_f)

    u3 = u.reshape(b, s, pw)
    pooled = _pool_mixer(u3, pool_hist, w_pool, pool_scale, pos0).reshape(m, pw)
    new_hist = u3[:, s - POOL_HIST:]

    logf_t = jnp.swapaxes(logf.reshape(b, s, nh), 1, 2)
    if past is None:
        assert b == 1
        _, *pieces = _cumsum_rows(logf_t.reshape(nh, s), with_pieces=True)
        attn = _prompt_attention(qb, kb, vb, _bias_features(pieces, True), _bias_features(pieces, False))
    else:
        cache_k, cache_v, cache_logf = past
        p = cache_k.shape[1]
        past_t = jnp.swapaxes(cache_logf, 1, 2)
        (c_all,) = _cumsum_rows(jnp.concatenate([past_t, logf_t], axis=-1).reshape(b * nh, p + s), False)
        c_all = c_all.reshape(b, nh, p + s)
        attn = _sample_attention(qb.reshape(b, s, aw), kb.reshape(b, s, aw), vb.reshape(b, s, aw),
                                 cache_k.reshape(b, p, aw), cache_v.reshape(b, p, aw),
                                 c_all[:, :, p:], c_all[:, :, :p]).reshape(m, aw)

    x1 = _out_proj(pooled, attn, w_out, x2)
    h2 = _rmsnorm(x1, mlp_norm_g)
    (hid,) = _proj(h2, w_up, 0, w_up.shape[1], "relu2")
    y = _down_proj(hid, w_down, x1)
    return (y.reshape(b, s, d), k.reshape(b, s, nh, HEAD_DIM), v.reshape(b, s, nh, HEAD_DIM),
            logf.reshape(b, s, nh), new_hist)


def kernel(x_prompt, x_sample, cache_k, cache_v, cache_logf, state_pool, attn_norm_g, w_in, b_f, q_norm_g,
           k_norm_g, w_pool, pool_scale, w_out, mlp_norm_g, w_up, w_down):
    depth = w_in.shape[0]
    pw = state_pool.shape[-1]
    nh = b_f.shape[-1]
    main_cols = w_in.shape[-1] - nh
    past_len = cache_k.shape[2]
    yp, ys = x_prompt, x_sample
    zero_hist = jnp.zeros((x_prompt.shape[0], POOL_HIST, pw), x_prompt.dtype)
    outs = [[] for _ in range(8)]
    for l in range(depth):
        w_in_b = w_in[l].astype(BF16)
        w_f = jnp.pad(w_in_b[:, main_cols:], ((0, 0), (0, HEAD_DIM - nh)))
        wts = (attn_norm_g[l], w_in_b, w_f, b_f[l], q_norm_g[l], k_norm_g[l], w_pool[l].astype(BF16),
               pool_scale[l], w_out[l].astype(BF16), mlp_norm_g[l], w_up[l].astype(BF16), w_down[l].astype(BF16))
        yp, k1, v1, f1, h1 = _layer(yp, zero_hist, None, wts, 0)
        ys, k2, v2, f2, h2 = _layer(ys, state_pool[l], (cache_k[l], cache_v[l], cache_logf[l]), wts, past_len)
        for acc, val in zip(outs, (k1, v1, f1, h1, k2, v2, f2, h2)):
            acc.append(val)
    return (yp, ys) + tuple(jnp.stack(o) for o in outs)
```

```python
import functools

import jax
import jax.numpy as jnp
from jax import lax
from jax.experimental import pallas as pl
from jax.experimental.pallas import tpu as pltpu

HEAD_DIM = 128
POOL_WINDOWS = (2, 4, 8, 16)
POOL_HIST = max(POOL_WINDOWS) - 1
HALO = POOL_HIST + 1
NORM_EPS = 1e-6
MASK_VALUE = -1e30
N_SPLIT = 3
LOG2E = 1.4426950408889634
ATTN_TQ = 2048
ATTN_TK = 1024
ATTN_QT = 256
ATTN_AHEAD = 2
MIB = 1024 * 1024

F32 = jnp.float32
BF16 = jnp.bfloat16


def _params(semantics, vmem_mib):
    return pltpu.CompilerParams(dimension_semantics=semantics, vmem_limit_bytes=vmem_mib * MIB)


def _rmsnorm_kernel(x_ref, g_ref, o_ref):
    x = x_ref[...]
    ms = jnp.mean(x * x, axis=-1, keepdims=True)
    o_ref[...] = (x * lax.rsqrt(ms + NORM_EPS) * g_ref[...]).astype(o_ref.dtype)


def _rmsnorm(x, g):
    m, d = x.shape
    tm = min(m, 256)
    return pl.pallas_call(
        _rmsnorm_kernel,
        out_shape=jax.ShapeDtypeStruct((m, d), BF16),
        grid=(m // tm,),
        in_specs=[pl.BlockSpec((tm, d), lambda i: (i, 0)), pl.BlockSpec((1, d), lambda i: (0, 0))],
        out_specs=pl.BlockSpec((tm, d), lambda i: (i, 0)),
        compiler_params=_params(("parallel",), 32),
        name="rmsnorm",
    )(x, g.reshape(1, d))


def _head_rmsnorm(a, g):
    ms = jnp.mean(a * a, axis=-1, keepdims=True)
    return a * lax.rsqrt(ms + NORM_EPS) * g


def _proj_kernel(x_ref, w_ref, *refs, mode, scale):
    acc = jnp.dot(x_ref[...], w_ref[...], preferred_element_type=F32)
    tn = acc.shape[1]
    if mode == "plain":
        (o_ref,) = refs
        o_ref[...] = acc
    elif mode == "both":
        o_ref, ob_ref = refs
        o_ref[...] = acc
        ob_ref[...] = acc.astype(BF16)
    elif mode == "both_t":
        o_ref, ot_ref = refs
        o_ref[...] = acc
        tkv = ot_ref.shape[3]
        for h in range(ot_ref.shape[0]):
            for c in range(ot_ref.shape[1]):
                tile = acc[c * tkv:(c + 1) * tkv, h * HEAD_DIM:(h + 1) * HEAD_DIM]
                ot_ref[h, c] = tile.T.astype(BF16)
    elif mode == "relu2":
        (ob_ref,) = refs
        r = jnp.maximum(acc, 0.0)
        ob_ref[...] = (r * r).astype(BF16)
    elif mode == "qnorm":
        g_ref, ob_ref = refs
        for h in range(tn // HEAD_DIM):
            sl = slice(h * HEAD_DIM, (h + 1) * HEAD_DIM)
            ob_ref[:, sl] = (_head_rmsnorm(acc[:, sl], g_ref[...]) * scale).astype(BF16)
    elif mode == "knorm":
        g_ref, o_ref, ob_ref = refs
        for h in range(tn // HEAD_DIM):
            sl = slice(h * HEAD_DIM, (h + 1) * HEAD_DIM)
            y = _head_rmsnorm(acc[:, sl], g_ref[...])
            o_ref[:, sl] = y
            ob_ref[:, sl] = y.astype(BF16)
    else:
        raise ValueError(mode)


def _proj(x, w, col_off, ncols, mode, g=None, scale=1.0, tn=512, tkv=ATTN_TK):
    m, k = x.shape
    tm = min(m, 1024)
    tn = min(tn, ncols)
    tkv = min(tkv, tm)
    assert m % tm == 0 and ncols % tn == 0 and col_off % tn == 0 and tm % tkv == 0
    joff = col_off // tn
    in_specs = [pl.BlockSpec((tm, k), lambda i, j: (i, 0)),
                pl.BlockSpec((k, tn), lambda i, j: (0, j + joff))]
    args = [x, w]
    if g is not None:
        in_specs.append(pl.BlockSpec((1, HEAD_DIM), lambda i, j: (0, 0)))
        args.append(g.reshape(1, HEAD_DIM))
    ospec = pl.BlockSpec((tm, tn), lambda i, j: (i, j))
    f32_out = jax.ShapeDtypeStruct((m, ncols), F32)
    bf_out = jax.ShapeDtypeStruct((m, ncols), BF16)
    t_out = jax.ShapeDtypeStruct((ncols // HEAD_DIM, m // tkv, HEAD_DIM, tkv), BF16)
    t_spec = pl.BlockSpec((tn // HEAD_DIM, tm // tkv, HEAD_DIM, tkv), lambda i, j: (j, i, 0, 0))
    out_shape, out_specs = {
        "plain": ([f32_out], [ospec]),
        "both": ([f32_out, bf_out], [ospec, ospec]),
        "both_t": ([f32_out, t_out], [ospec, t_spec]),
        "relu2": ([bf_out], [ospec]),
        "qnorm": ([bf_out], [ospec]),
        "knorm": ([f32_out, bf_out], [ospec, ospec]),
    }[mode]
    return pl.pallas_call(
        functools.partial(_proj_kernel, mode=mode, scale=scale),
        out_shape=out_shape,
        grid=(m // tm, ncols // tn),
        in_specs=in_specs,
        out_specs=out_specs,
        compiler_params=_params(("parallel", "arbitrary"), 48),
        name="proj_" + mode,
    )(*args)


def _forget_kernel(x_ref, w_ref, b_ref, o_ref):
    nh = o_ref.shape[1]
    z = jnp.dot(x_ref[...], w_ref[...], preferred_element_type=F32)[:, :nh] + b_ref[...]
    o_ref[...] = jnp.minimum(z, 0.0) - jnp.log(1.0 + jnp.exp(-jnp.abs(z)))


def _forget_proj(x, w_f, b_f):
    m, k = x.shape
    nh = b_f.shape[0]
    tm = min(m, 1024)
    return pl.pallas_call(
        _forget_kernel,
        out_shape=jax.ShapeDtypeStruct((m, nh), F32),
        grid=(m // tm,),
        in_specs=[pl.BlockSpec((tm, k), lambda i: (i, 0)),
                  pl.BlockSpec((k, HEAD_DIM), lambda i: (0, 0)),
                  pl.BlockSpec((1, nh), lambda i: (0, 0))],
        out_specs=pl.BlockSpec((tm, nh), lambda i: (i, 0)),
        compiler_params=_params(("parallel",), 32),
        name="forget_proj",
    )(x, w_f, b_f.reshape(1, nh))


def _pool_kernel(u_ref, prev_ref, hist_ref, w_ref, scale_ref, o_ref, ext_ref, *, ts, pos0, group):
    i = pl.program_id(1)

    @pl.when(i == 0)
    def _():
        ext_ref[0:HALO, :] = hist_ref[...]

    @pl.when(i > 0)
    def _():
        ext_ref[0:HALO, :] = prev_ref[...]

    ext_ref[HALO:HALO + ts, :] = u_ref[...]
    pos = lax.broadcasted_iota(jnp.int32, (ts, 1), 0) + (i * ts + pos0)
    for gi, win in enumerate(POOL_WINDOWS):
        lo = gi * group
        cur = ext_ref[HALO:HALO + ts, lo:lo + group]
        tot = cur
        for j in range(1, win):
            tot = tot + ext_ref[HALO - j:HALO - j + ts, lo:lo + group]
        cnt = jnp.minimum(pos + 1, win).astype(F32)
        d = tot / cnt - cur
        out = jnp.dot(d.astype(BF16), w_ref[gi], preferred_element_type=F32)
        o_ref[:, lo:lo + group] = (out * scale_ref[:, lo:lo + group]).astype(BF16)


def _pool_mixer(u, hist, w_pool, pool_scale, pos0):
    b, s, c = u.shape
    group = c // len(POOL_WINDOWS)
    ts = min(s, 512)
    assert s % ts == 0 and ts % HALO == 0
    hist16 = jnp.pad(hist, ((0, 0), (HALO - POOL_HIST, 0), (0, 0)))
    nprev = ts // HALO
    return pl.pallas_call(
        functools.partial(_pool_kernel, ts=ts, pos0=pos0, group=group),
        out_shape=jax.ShapeDtypeStruct((b, s, c), BF16),
        grid=(b, s // ts),
        in_specs=[pl.BlockSpec((None, ts, c), lambda bi, i: (bi, i, 0)),
                  pl.BlockSpec((None, HALO, c), lambda bi, i: (bi, jnp.maximum(i * nprev - 1, 0), 0)),
                  pl.BlockSpec((None, HALO, c), lambda bi, i: (bi, 0, 0)),
                  pl.BlockSpec(w_pool.shape, lambda bi, i: (0, 0, 0)),
                  pl.BlockSpec((1, c), lambda bi, i: (0, 0))],
        out_specs=pl.BlockSpec((None, ts, c), lambda bi, i: (bi, i, 0)),
        scratch_shapes=[pltpu.VMEM((HALO + ts, c), F32)],
        compiler_params=_params(("parallel", "arbitrary"), 40),
        name="pool_mixer",
    )(u, u, hist16, w_pool, pool_scale.reshape(1, c))


def _cumsum_kernel(x_ref, c_ref, *piece_refs):
    x = x_ref[...]
    n = x.shape[1]
    lane = lax.broadcasted_iota(jnp.int32, x.shape, 1)
    shift = 1
    while shift < n:
        x = x + jnp.where(lane >= shift, pltpu.roll(x, shift, 1), 0.0)
        shift *= 2
    c_ref[...] = x
    rem = x * LOG2E
    for p_ref in piece_refs:
        piece = rem.astype(BF16)
        p_ref[...] = piece
        rem = rem - piece.astype(F32)


def _cumsum_rows(x, with_pieces):
    r, n = x.shape
    npad = -n % HEAD_DIM
    xp = jnp.pad(x, ((0, 0), (0, npad)))
    shp = xp.shape
    out_shape = [jax.ShapeDtypeStruct(shp, F32)]
    if with_pieces:
        out_shape += [jax.ShapeDtypeStruct(shp, BF16)] * N_SPLIT
    spec = pl.BlockSpec(shp, lambda: (0, 0))
    outs = pl.pallas_call(
        _cumsum_kernel,
        out_shape=out_shape,
        in_specs=[spec],
        out_specs=[spec] * len(out_shape),
        compiler_params=pltpu.CompilerParams(vmem_limit_bytes=32 * MIB),
        name="forget_cumsum",
    )(xp)
    return [o[:, :n] for o in outs]


_NT = (((1,), (1,)), ((), ()))


def _softmax_step(s, v, m_prev, acc_prev):
    m_new = jnp.maximum(m_prev, jnp.max(s, axis=1, keepdims=True))
    p = jnp.exp2(s - m_new).astype(BF16)
    v_ones = jnp.concatenate([v, jnp.ones(v.shape, BF16)], axis=1)
    pv = jnp.dot(p, v_ones, preferred_element_type=F32)
    return m_new, jnp.exp2(m_prev - m_new) * acc_prev + pv


def _attn_kernel(q_ref, qa_ref, k_ref, ka_ref, vt_ref, o_ref, m_ref, l_ref, acc_ref, *, tq, tk, qt):
    qi = pl.program_id(1)
    m_ref[...] = jnp.full(m_ref.shape, MASK_VALUE, F32)
    l_ref[...] = jnp.zeros(l_ref.shape, F32)
    acc_ref[...] = jnp.zeros(acc_ref.shape, F32)

    def step(j, diag):
        off = pl.multiple_of(j * tk, tk)
        kc = jnp.concatenate([k_ref[pl.ds(off, tk), :], ka_ref[pl.ds(off, tk), :]], axis=1)
        vt = vt_ref[j]
        groups = [t for t in range(tq // qt) if diag is None or (t + 1) * qt > diag * tk]

        def nkeys(t):
            return tk if diag is None else min((t + 1) * qt - diag * tk, tk)

        def scores(t):
            qs = slice(t * qt, (t + 1) * qt)
            nk = nkeys(t)
            qc = jnp.concatenate([q_ref[qs, :], qa_ref[qs, :]], axis=1)
            s = lax.dot_general(kc[:nk], qc, _NT, preferred_element_type=F32)
            if diag is not None and (t + 1) * qt <= (diag + 1) * tk:
                key = off + lax.broadcasted_iota(jnp.int32, (nk, 1), 0)
                qry = qi * tq + t * qt + lax.broadcasted_iota(jnp.int32, (1, qt), 1)
                s = jnp.where(key <= qry, s, MASK_VALUE)
            return s

        pending = [scores(t) for t in groups[:ATTN_AHEAD]]
        for gi, t in enumerate(groups):
            qs = slice(t * qt, (t + 1) * qt)
            s = pending.pop(0)
            if gi + ATTN_AHEAD < len(groups):
                pending.append(scores(groups[gi + ATTN_AHEAD]))
            m_prev = m_ref[:, qs]
            m_new = jnp.maximum(m_prev, jnp.max(s, axis=0, keepdims=True))
            p = jnp.exp2(s - m_new)
            alpha = jnp.exp2(m_prev - m_new)
            pv = jnp.dot(vt[:, :nkeys(t)], p.astype(BF16), preferred_element_type=F32)
            l_ref[:, qs] = alpha * l_ref[:, qs] + jnp.sum(p, axis=0, keepdims=True)
            acc_ref[:, qs] = alpha * acc_ref[:, qs] + pv
            m_ref[:, qs] = m_new

    per_q = tq // tk

    def body(j, carry):
        step(j, None)
        return carry

    lax.fori_loop(0, qi * per_q, body, 0)
    for d in range(per_q):
        step(qi * per_q + d, d)
    o_ref[...] = (acc_ref[...] / l_ref[...]).T.astype(o_ref.dtype)


def _prompt_attention(q, k, vt, qa, ka, tq=ATTN_TQ, qt=ATTN_QT):
    s, width = q.shape
    nh, nkv, _, tk = vt.shape
    tq = min(tq, s)
    qt = min(qt, tq)
    assert s % tq == 0 and tq % tk == 0 and tq % qt == 0 and nkv * tk == s
    row = lambda h, i: (i, h)
    full = lambda h, i: (0, h)
    return pl.pallas_call(
        functools.partial(_attn_kernel, tq=tq, tk=tk, qt=qt),
        out_shape=jax.ShapeDtypeStruct((s, width), BF16),
        grid=(nh, s // tq),
        in_specs=[pl.BlockSpec((tq, HEAD_DIM), row),
                  pl.BlockSpec((None, tq, HEAD_DIM), lambda h, i: (h, i, 0)),
                  pl.BlockSpec((s, HEAD_DIM), full),
                  pl.BlockSpec((None, s, HEAD_DIM), lambda h, i: (h, 0, 0)),
                  pl.BlockSpec((None, nkv, HEAD_DIM, tk), lambda h, i: (h, 0, 0, 0))],
        out_specs=pl.BlockSpec((tq, HEAD_DIM), row),
        scratch_shapes=[pltpu.VMEM((1, tq), F32), pltpu.VMEM((1, tq), F32), pltpu.VMEM((HEAD_DIM, tq), F32)],
        compiler_params=_params(("parallel", "arbitrary"), 48),
        name="prompt_attention",
    )(q, qa, k, ka, vt)


def _bias_features(pieces, query_side):
    c = jnp.stack(pieces, axis=-1)
    ones = jnp.ones_like(c)
    feats = jnp.concatenate([c, ones] if query_side else [ones, -c], axis=-1)
    return jnp.pad(feats, ((0, 0), (0, 0), (0, HEAD_DIM - 2 * N_SPLIT)))


def _sample_attn_kernel(q_ref, kn_ref, vn_ref, kc_ref, vc_ref, cq_ref, cn_ref, cp_ref, o_ref, m_ref, acc_ref,
                        *, nh, tp):
    c = pl.program_id(1)
    sq = q_ref.shape[0]

    @pl.when(c == 0)
    def _():
        m_ref[...] = jnp.full(m_ref.shape, MASK_VALUE, F32)
        acc_ref[...] = jnp.zeros(acc_ref.shape, F32)

    for h in range(nh):
        sl = slice(h * HEAD_DIM, (h + 1) * HEAD_DIM)
        kh = kc_ref[pl.ds(h, tp, stride=nh), :].astype(BF16)
        vh = vc_ref[pl.ds(h, tp, stride=nh), :].astype(BF16)
        s = lax.dot_general(q_ref[:, sl], kh, _NT, preferred_element_type=F32)
        s = s + (cq_ref[:, h:h + 1] - cp_ref[h:h + 1, :]) * LOG2E
        m_ref[h], acc_ref[h] = _softmax_step(s, vh, m_ref[h], acc_ref[h])

    @pl.when(c == pl.num_programs(1) - 1)
    def _():
        rows = lax.broadcasted_iota(jnp.int32, (sq, sq), 0)
        cols = lax.broadcasted_iota(jnp.int32, (sq, sq), 1)
        for h in range(nh):
            sl = slice(h * HEAD_DIM, (h + 1) * HEAD_DIM)
            s = lax.dot_general(q_ref[:, sl], kn_ref[:, sl], _NT, preferred_element_type=F32)
            s = s + (cq_ref[:, h:h + 1] - cn_ref[h:h + 1, :]) * LOG2E
            s = jnp.where(cols <= rows, s, MASK_VALUE)
            _, acc = _softmax_step(s, vn_ref[:, sl], m_ref[h], acc_ref[h])
            o_ref[:, sl] = (acc[:, :HEAD_DIM] / acc[:, HEAD_DIM:]).astype(o_ref.dtype)


def _sample_attention(q, kn, vn, cache_k, cache_v, layer, c_new, c_past, tp=512):
    b, sq, width = q.shape
    nh = width // HEAD_DIM
    p = cache_k.shape[2]
    tp = min(tp, p)
    assert p % tp == 0
    nchunk = p // tp
    rows_k = cache_k.reshape(-1, HEAD_DIM)
    rows_v = cache_v.reshape(-1, HEAD_DIM)
    new_spec = pl.BlockSpec((None, sq, width), lambda bi, c: (bi, 0, 0))
    cache_spec = pl.BlockSpec((tp * nh, HEAD_DIM), lambda bi, c: ((layer * b + bi) * nchunk + c, 0))
    return pl.pallas_call(
        functools.partial(_sample_attn_kernel, nh=nh, tp=tp),
        out_shape=jax.ShapeDtypeStruct((b, sq, width), BF16),
        grid=(b, nchunk),
        in_specs=[new_spec, new_spec, new_spec, cache_spec, cache_spec,
                  pl.BlockSpec((None, sq, nh), lambda bi, c: (bi, 0, 0)),
                  pl.BlockSpec((None, nh, sq), lambda bi, c: (bi, 0, 0)),
                  pl.BlockSpec((None, nh, tp), lambda bi, c: (bi, 0, c))],
        out_specs=new_spec,
        scratch_shapes=[pltpu.VMEM((nh, sq, 1), F32), pltpu.VMEM((nh, sq, 2 * HEAD_DIM), F32)],
        compiler_params=_params(("parallel", "arbitrary"), 40),
        name="sample_attention",
    )(q, kn, vn, rows_k, rows_v, jnp.swapaxes(c_new, 1, 2), c_new, c_past)


def _outproj_kernel(p_ref, a_ref, wt_ref, wb_ref, r_ref, o_ref):
    acc = jnp.dot(p_ref[...], wt_ref[...], preferred_element_type=F32)
    acc = acc + jnp.dot(a_ref[...], wb_ref[...], preferred_element_type=F32)
    o_ref[...] = r_ref[...] + acc


def _out_proj(pooled, attn, w_out, resid, tn=512):
    m, half = pooled.shape
    assert attn.shape == (m, half) and w_out.shape[0] == 2 * half
    n = w_out.shape[1]
    tm = min(m, 1024)
    tn = min(tn, n)
    xspec = pl.BlockSpec((tm, half), lambda i, j: (i, 0))
    ospec = pl.BlockSpec((tm, tn), lambda i, j: (i, j))
    return pl.pallas_call(
        _outproj_kernel,
        out_shape=jax.ShapeDtypeStruct((m, n), F32),
        grid=(m // tm, n // tn),
        in_specs=[xspec, xspec,
                  pl.BlockSpec((half, tn), lambda i, j: (0, j)),
                  pl.BlockSpec((half, tn), lambda i, j: (1, j)),
                  ospec],
        out_specs=ospec,
        compiler_params=_params(("parallel", "arbitrary"), 48),
        name="out_proj",
    )(pooled, attn, w_out, w_out, resid)


def _down_kernel(x_ref, w_ref, r_ref, o_ref):
    kk = pl.program_id(2)
    d = jnp.dot(x_ref[...], w_ref[...], preferred_element_type=F32)

    @pl.when(kk == 0)
    def _():
        o_ref[...] = r_ref[...] + d

    @pl.when(kk > 0)
    def _():
        o_ref[...] += d


def _down_proj(x, w, resid, tn=512, tk=4096):
    m, f = x.shape
    n = w.shape[1]
    tm = min(m, 1024)
    tn, tk = min(tn, n), min(tk, f)
    ospec = pl.BlockSpec((tm, tn), lambda i, j, kk: (i, j))
    return pl.pallas_call(
        _down_kernel,
        out_shape=jax.ShapeDtypeStruct((m, n), F32),
        grid=(m // tm, n // tn, f // tk),
        in_specs=[pl.BlockSpec((tm, tk), lambda i, j, kk: (i, kk)),
                  pl.BlockSpec((tk, tn), lambda i, j, kk: (kk, j)),
                  ospec],
        out_specs=ospec,
        compiler_params=_params(("parallel", "arbitrary", "arbitrary"), 48),
        name="down_proj",
    )(x, w, resid)


def _layer(x, pool_hist, past, wts, pos0):
    (attn_norm_g, w_in, w_f, b_f, q_norm_g, k_norm_g, w_pool, pool_scale, w_out, mlp_norm_g, w_up, w_down) = wts
    b, s, d = x.shape
    m = b * s
    pw = pool_scale.shape[0]
    aw = d - pw
    nh = aw // HEAD_DIM
    x2 = x.reshape(m, d)

    h = _rmsnorm(x2, attn_norm_g)
    (u,) = _proj(h, w_in, 0, pw, "plain")
    (qb,) = _proj(h, w_in, pw, aw, "qnorm", g=q_norm_g, scale=HEAD_DIM ** -0.5 * LOG2E)
    k, kb = _proj(h, w_in, pw + aw, aw, "knorm", g=k_norm_g)
    v, vb = _proj(h, w_in, pw + 2 * aw, aw, "both_t" if past is None else "both")
    logf = _forget_proj(h, w_f, b_f)

    u3 = u.reshape(b, s, pw)
    pooled = _pool_mixer(u3, pool_hist, w_pool, pool_scale, pos0).reshape(m, pw)
    new_hist = u3[:, s - POOL_HIST:]

    logf_t = jnp.swapaxes(logf.reshape(b, s, nh), 1, 2)
    if past is None:
        assert b == 1
        _, *pieces = _cumsum_rows(logf_t.reshape(nh, s), with_pieces=True)
        attn = _prompt_attention(qb, kb, vb, _bias_features(pieces, True), _bias_features(pieces, False))
    else:
        cache_k, cache_v, cache_logf, layer = past
        p = cache_k.shape[2]
        past_t = jnp.swapaxes(cache_logf[layer], 1, 2)
        (c_all,) = _cumsum_rows(jnp.concatenate([past_t, logf_t], axis=-1).reshape(b * nh, p + s), False)
        c_all = c_all.reshape(b, nh, p + s)
        attn = _sample_attention(qb.reshape(b, s, aw), kb.reshape(b, s, aw), vb.reshape(b, s, aw),
                                 cache_k, cache_v, layer, c_all[:, :, p:], c_all[:, :, :p]).reshape(m, aw)

    x1 = _out_proj(pooled, attn, w_out, x2)
    h2 = _rmsnorm(x1, mlp_norm_g)
    (hid,) = _proj(h2, w_up, 0, w_up.shape[1], "relu2")
    y = _down_proj(hid, w_down, x1)
    return (y.reshape(b, s, d), k.reshape(b, s, nh, HEAD_DIM), v.reshape(b, s, nh, HEAD_DIM),
            logf.reshape(b, s, nh), new_hist)


def kernel(x_prompt, x_sample, cache_k, cache_v, cache_logf, state_pool, attn_norm_g, w_in, b_f, q_norm_g,
           k_norm_g, w_pool, pool_scale, w_out, mlp_norm_g, w_up, w_down):
    depth = w_in.shape[0]
    pw = state_pool.shape[-1]
    nh = b_f.shape[-1]
    main_cols = w_in.shape[-1] - nh
    past_len = cache_k.shape[2]
    yp, ys = x_prompt, x_sample
    zero_hist = jnp.zeros((x_prompt.shape[0], POOL_HIST, pw), x_prompt.dtype)
    outs = [[] for _ in range(8)]
    for l in range(depth):
        w_in_b = w_in[l].astype(BF16)
        w_f = jnp.pad(w_in_b[:, main_cols:], ((0, 0), (0, HEAD_DIM - nh)))
        wts = (attn_norm_g[l], w_in_b, w_f, b_f[l], q_norm_g[l], k_norm_g[l], w_pool[l].astype(BF16),
               pool_scale[l], w_out[l].astype(BF16), mlp_norm_g[l], w_up[l].astype(BF16), w_down[l].astype(BF16))
        yp, k1, v1, f1, h1 = _layer(yp, zero_hist, None, wts, 0)
        ys, k2, v2, f2, h2 = _layer(ys, state_pool[l], (cache_k, cache_v, cache_logf, l), wts, past_len)
        for acc, val in zip(outs, (k1, v1, f1, h1, k2, v2, f2, h2)):
            acc.append(val)
    return (yp, ys) + tuple(jnp.stack(o) for o in outs)
```

```python
import functools

import jax
import jax.numpy as jnp
from jax import lax
from jax.experimental import pallas as pl
from jax.experimental.pallas import tpu as pltpu

HEAD_DIM = 128
POOL_WINDOWS = (2, 4, 8, 16)
POOL_HIST = max(POOL_WINDOWS) - 1
HALO = POOL_HIST + 1
NORM_EPS = 1e-6
MASK_VALUE = -1e30
N_SPLIT = 3
LOG2E = 1.4426950408889634
ATTN_TQ = 2048
ATTN_TK = 1024
ATTN_QT = 256
ATTN_AHEAD = 2
MIB = 1024 * 1024

F32 = jnp.float32
BF16 = jnp.bfloat16


def _params(semantics, vmem_mib):
    return pltpu.CompilerParams(dimension_semantics=semantics, vmem_limit_bytes=vmem_mib * MIB)


def _rmsnorm_kernel(x_ref, g_ref, o_ref):
    x = x_ref[...]
    ms = jnp.mean(x * x, axis=-1, keepdims=True)
    o_ref[...] = (x * lax.rsqrt(ms + NORM_EPS) * g_ref[...]).astype(o_ref.dtype)


def _rmsnorm(x, g):
    m, d = x.shape
    tm = min(m, 256)
    return pl.pallas_call(
        _rmsnorm_kernel,
        out_shape=jax.ShapeDtypeStruct((m, d), BF16),
        grid=(m // tm,),
        in_specs=[pl.BlockSpec((tm, d), lambda i: (i, 0)), pl.BlockSpec((1, d), lambda i: (0, 0))],
        out_specs=pl.BlockSpec((tm, d), lambda i: (i, 0)),
        compiler_params=_params(("parallel",), 32),
        name="rmsnorm",
    )(x, g.reshape(1, d))


def _head_rmsnorm(a, g):
    ms = jnp.mean(a * a, axis=-1, keepdims=True)
    return a * lax.rsqrt(ms + NORM_EPS) * g


def _proj_kernel(x_ref, w_ref, *refs, mode, scale):
    acc = jnp.dot(x_ref[...], w_ref[...].astype(BF16), preferred_element_type=F32)
    tn = acc.shape[1]
    if mode == "plain":
        (o_ref,) = refs
        o_ref[...] = acc
    elif mode == "both":
        o_ref, ob_ref = refs
        o_ref[...] = acc
        ob_ref[...] = acc.astype(BF16)
    elif mode == "both_t":
        o_ref, ot_ref = refs
        o_ref[...] = acc
        tkv = ot_ref.shape[3]
        for h in range(ot_ref.shape[0]):
            for c in range(ot_ref.shape[1]):
                tile = acc[c * tkv:(c + 1) * tkv, h * HEAD_DIM:(h + 1) * HEAD_DIM]
                ot_ref[h, c] = tile.T.astype(BF16)
    elif mode == "relu2":
        (ob_ref,) = refs
        r = jnp.maximum(acc, 0.0)
        ob_ref[...] = (r * r).astype(BF16)
    elif mode == "qnorm":
        g_ref, ob_ref = refs
        for h in range(tn // HEAD_DIM):
            sl = slice(h * HEAD_DIM, (h + 1) * HEAD_DIM)
            ob_ref[:, sl] = (_head_rmsnorm(acc[:, sl], g_ref[...]) * scale).astype(BF16)
    elif mode == "knorm":
        g_ref, o_ref, ob_ref = refs
        for h in range(tn // HEAD_DIM):
            sl = slice(h * HEAD_DIM, (h + 1) * HEAD_DIM)
            y = _head_rmsnorm(acc[:, sl], g_ref[...])
            o_ref[:, sl] = y
            ob_ref[:, sl] = y.astype(BF16)
    else:
        raise ValueError(mode)


def _proj(x, w, col_off, ncols, mode, g=None, scale=1.0, tn=512, tkv=ATTN_TK):
    m, k = x.shape
    tm = min(m, 1024)
    tn = min(tn, ncols)
    tkv = min(tkv, tm)
    assert m % tm == 0 and ncols % tn == 0 and col_off % tn == 0 and tm % tkv == 0
    joff = col_off // tn
    in_specs = [pl.BlockSpec((tm, k), lambda i, j: (i, 0)),
                pl.BlockSpec((k, tn), lambda i, j: (0, j + joff))]
    args = [x, w]
    if g is not None:
        in_specs.append(pl.BlockSpec((1, HEAD_DIM), lambda i, j: (0, 0)))
        args.append(g.reshape(1, HEAD_DIM))
    ospec = pl.BlockSpec((tm, tn), lambda i, j: (i, j))
    f32_out = jax.ShapeDtypeStruct((m, ncols), F32)
    bf_out = jax.ShapeDtypeStruct((m, ncols), BF16)
    t_out = jax.ShapeDtypeStruct((ncols // HEAD_DIM, m // tkv, HEAD_DIM, tkv), BF16)
    t_spec = pl.BlockSpec((tn // HEAD_DIM, tm // tkv, HEAD_DIM, tkv), lambda i, j: (j, i, 0, 0))
    out_shape, out_specs = {
        "plain": ([f32_out], [ospec]),
        "both": ([f32_out, bf_out], [ospec, ospec]),
        "both_t": ([f32_out, t_out], [ospec, t_spec]),
        "relu2": ([bf_out], [ospec]),
        "qnorm": ([bf_out], [ospec]),
        "knorm": ([f32_out, bf_out], [ospec, ospec]),
    }[mode]
    return pl.pallas_call(
        functools.partial(_proj_kernel, mode=mode, scale=scale),
        out_shape=out_shape,
        grid=(m // tm, ncols // tn),
        in_specs=in_specs,
        out_specs=out_specs,
        compiler_params=_params(("parallel", "arbitrary"), 56),
        name="proj_" + mode,
    )(*args)


def _forget_kernel(x_ref, w_ref, b_ref, o_ref):
    nh = o_ref.shape[1]
    z = jnp.dot(x_ref[...], w_ref[...], preferred_element_type=F32)[:, :nh] + b_ref[...]
    o_ref[...] = jnp.minimum(z, 0.0) - jnp.log(1.0 + jnp.exp(-jnp.abs(z)))


def _forget_proj(x, w_f, b_f):
    m, k = x.shape
    nh = b_f.shape[0]
    tm = min(m, 1024)
    return pl.pallas_call(
        _forget_kernel,
        out_shape=jax.ShapeDtypeStruct((m, nh), F32),
        grid=(m // tm,),
        in_specs=[pl.BlockSpec((tm, k), lambda i: (i, 0)),
                  pl.BlockSpec((k, HEAD_DIM), lambda i: (0, 0)),
                  pl.BlockSpec((1, nh), lambda i: (0, 0))],
        out_specs=pl.BlockSpec((tm, nh), lambda i: (i, 0)),
        compiler_params=_params(("parallel",), 32),
        name="forget_proj",
    )(x, w_f, b_f.reshape(1, nh))


def _pool_kernel(u_ref, prev_ref, hist_ref, w_ref, scale_ref, o_ref, ext_ref, *, ts, pos0, group):
    i = pl.program_id(1)

    @pl.when(i == 0)
    def _():
        ext_ref[0:HALO, :] = hist_ref[...]

    @pl.when(i > 0)
    def _():
        ext_ref[0:HALO, :] = prev_ref[...]

    ext_ref[HALO:HALO + ts, :] = u_ref[...]
    pos = lax.broadcasted_iota(jnp.int32, (ts, 1), 0) + (i * ts + pos0)
    for gi, win in enumerate(POOL_WINDOWS):
        lo = gi * group
        cur = ext_ref[HALO:HALO + ts, lo:lo + group]
        tot = cur
        for j in range(1, win):
            tot = tot + ext_ref[HALO - j:HALO - j + ts, lo:lo + group]
        cnt = jnp.minimum(pos + 1, win).astype(F32)
        d = tot / cnt - cur
        out = jnp.dot(d.astype(BF16), w_ref[gi], preferred_element_type=F32)
        o_ref[:, lo:lo + group] = (out * scale_ref[:, lo:lo + group]).astype(BF16)


def _pool_mixer(u, hist, w_pool, pool_scale, pos0):
    b, s, c = u.shape
    group = c // len(POOL_WINDOWS)
    ts = min(s, 512)
    assert s % ts == 0 and ts % HALO == 0
    hist16 = jnp.pad(hist, ((0, 0), (HALO - POOL_HIST, 0), (0, 0)))
    nprev = ts // HALO
    return pl.pallas_call(
        functools.partial(_pool_kernel, ts=ts, pos0=pos0, group=group),
        out_shape=jax.ShapeDtypeStruct((b, s, c), BF16),
        grid=(b, s // ts),
        in_specs=[pl.BlockSpec((None, ts, c), lambda bi, i: (bi, i, 0)),
                  pl.BlockSpec((None, HALO, c), lambda bi, i: (bi, jnp.maximum(i * nprev - 1, 0), 0)),
                  pl.BlockSpec((None, HALO, c), lambda bi, i: (bi, 0, 0)),
                  pl.BlockSpec(w_pool.shape, lambda bi, i: (0, 0, 0)),
                  pl.BlockSpec((1, c), lambda bi, i: (0, 0))],
        out_specs=pl.BlockSpec((None, ts, c), lambda bi, i: (bi, i, 0)),
        scratch_shapes=[pltpu.VMEM((HALO + ts, c), F32)],
        compiler_params=_params(("parallel", "arbitrary"), 40),
        name="pool_mixer",
    )(u, u, hist16, w_pool, pool_scale.reshape(1, c))


def _cumsum_kernel(x_ref, c_ref, *piece_refs):
    x = x_ref[...]
    n = x.shape[1]
    lane = lax.broadcasted_iota(jnp.int32, x.shape, 1)
    shift = 1
    while shift < n:
        x = x + jnp.where(lane >= shift, pltpu.roll(x, shift, 1), 0.0)
        shift *= 2
    c_ref[...] = x
    rem = x * LOG2E
    for p_ref in piece_refs:
        piece = rem.astype(BF16)
        p_ref[...] = piece
        rem = rem - piece.astype(F32)


def _cumsum_rows(x, with_pieces):
    r, n = x.shape
    npad = -n % HEAD_DIM
    xp = jnp.pad(x, ((0, 0), (0, npad)))
    shp = xp.shape
    out_shape = [jax.ShapeDtypeStruct(shp, F32)]
    if with_pieces:
        out_shape += [jax.ShapeDtypeStruct(shp, BF16)] * N_SPLIT
    spec = pl.BlockSpec(shp, lambda: (0, 0))
    outs = pl.pallas_call(
        _cumsum_kernel,
        out_shape=out_shape,
        in_specs=[spec],
        out_specs=[spec] * len(out_shape),
        compiler_params=pltpu.CompilerParams(vmem_limit_bytes=32 * MIB),
        name="forget_cumsum",
    )(xp)
    return [o[:, :n] for o in outs]


_NT = (((1,), (1,)), ((), ()))


def _softmax_step(s, v, m_prev, acc_prev):
    m_new = jnp.maximum(m_prev, jnp.max(s, axis=1, keepdims=True))
    p = jnp.exp2(s - m_new).astype(BF16)
    v_ones = jnp.concatenate([v, jnp.ones(v.shape, BF16)], axis=1)
    pv = jnp.dot(p, v_ones, preferred_element_type=F32)
    return m_new, jnp.exp2(m_prev - m_new) * acc_prev + pv


def _attn_kernel(q_ref, qa_ref, k_ref, ka_ref, vt_ref, o_ref, m_ref, l_ref, acc_ref, *, tq, tk, qt):
    qi = pl.program_id(1)
    m_ref[...] = jnp.full(m_ref.shape, MASK_VALUE, F32)
    l_ref[...] = jnp.zeros(l_ref.shape, F32)
    acc_ref[...] = jnp.zeros(acc_ref.shape, F32)

    def step(j, diag):
        off = pl.multiple_of(j * tk, tk)
        kc = jnp.concatenate([k_ref[pl.ds(off, tk), :], ka_ref[pl.ds(off, tk), :]], axis=1)
        vt = vt_ref[j]
        groups = [t for t in range(tq // qt) if diag is None or (t + 1) * qt > diag * tk]

        def nkeys(t):
            return tk if diag is None else min((t + 1) * qt - diag * tk, tk)

        def scores(t):
            qs = slice(t * qt, (t + 1) * qt)
            nk = nkeys(t)
            qc = jnp.concatenate([q_ref[qs, :], qa_ref[qs, :]], axis=1)
            s = lax.dot_general(kc[:nk], qc, _NT, preferred_element_type=F32)
            if diag is not None and (t + 1) * qt <= (diag + 1) * tk:
                key = off + lax.broadcasted_iota(jnp.int32, (nk, 1), 0)
                qry = qi * tq + t * qt + lax.broadcasted_iota(jnp.int32, (1, qt), 1)
                s = jnp.where(key <= qry, s, MASK_VALUE)
            return s

        pending = [scores(t) for t in groups[:ATTN_AHEAD]]
        for gi, t in enumerate(groups):
            qs = slice(t * qt, (t + 1) * qt)
            s = pending.pop(0)
            if gi + ATTN_AHEAD < len(groups):
                pending.append(scores(groups[gi + ATTN_AHEAD]))
            m_prev = m_ref[:, qs]
            m_new = jnp.maximum(m_prev, jnp.max(s, axis=0, keepdims=True))
            p = jnp.exp2(s - m_new)
            alpha = jnp.exp2(m_prev - m_new)
            pv = jnp.dot(vt[:, :nkeys(t)], p.astype(BF16), preferred_element_type=F32)
            l_ref[:, qs] = alpha * l_ref[:, qs] + jnp.sum(p, axis=0, keepdims=True)
            acc_ref[:, qs] = alpha * acc_ref[:, qs] + pv
            m_ref[:, qs] = m_new

    per_q = tq // tk

    def body(j, carry):
        step(j, None)
        return carry

    lax.fori_loop(0, qi * per_q, body, 0)
    for d in range(per_q):
        step(qi * per_q + d, d)
    o_ref[...] = (acc_ref[...] / l_ref[...]).T.astype(o_ref.dtype)


def _prompt_attention(q, k, vt, qa, ka, tq=ATTN_TQ, qt=ATTN_QT):
    s, width = q.shape
    nh, nkv, _, tk = vt.shape
    tq = min(tq, s)
    qt = min(qt, tq)
    assert s % tq == 0 and tq % tk == 0 and tq % qt == 0 and nkv * tk == s
    row = lambda h, i: (i, h)
    full = lambda h, i: (0, h)
    return pl.pallas_call(
        functools.partial(_attn_kernel, tq=tq, tk=tk, qt=qt),
        out_shape=jax.ShapeDtypeStruct((s, width), BF16),
        grid=(nh, s // tq),
        in_specs=[pl.BlockSpec((tq, HEAD_DIM), row),
                  pl.BlockSpec((None, tq, HEAD_DIM), lambda h, i: (h, i, 0)),
                  pl.BlockSpec((s, HEAD_DIM), full),
                  pl.BlockSpec((None, s, HEAD_DIM), lambda h, i: (h, 0, 0)),
                  pl.BlockSpec((None, nkv, HEAD_DIM, tk), lambda h, i: (h, 0, 0, 0))],
        out_specs=pl.BlockSpec((tq, HEAD_DIM), row),
        scratch_shapes=[pltpu.VMEM((1, tq), F32), pltpu.VMEM((1, tq), F32), pltpu.VMEM((HEAD_DIM, tq), F32)],
        compiler_params=_params(("parallel", "arbitrary"), 48),
        name="prompt_attention",
    )(q, qa, k, ka, vt)


def _bias_features(pieces, query_side):
    c = jnp.stack(pieces, axis=-1)
    ones = jnp.ones_like(c)
    feats = jnp.concatenate([c, ones] if query_side else [ones, -c], axis=-1)
    return jnp.pad(feats, ((0, 0), (0, 0), (0, HEAD_DIM - 2 * N_SPLIT)))


def _sample_attn_kernel(q_ref, kn_ref, vn_ref, kc_ref, vc_ref, cq_ref, cn_ref, cp_ref, o_ref, m_ref, acc_ref,
                        *, nh, tp):
    c = pl.program_id(1)
    sq = q_ref.shape[0]

    @pl.when(c == 0)
    def _():
        m_ref[...] = jnp.full(m_ref.shape, MASK_VALUE, F32)
        acc_ref[...] = jnp.zeros(acc_ref.shape, F32)

    for h in range(nh):
        sl = slice(h * HEAD_DIM, (h + 1) * HEAD_DIM)
        kh = kc_ref[pl.ds(h, tp, stride=nh), :].astype(BF16)
        vh = vc_ref[pl.ds(h, tp, stride=nh), :].astype(BF16)
        s = lax.dot_general(q_ref[:, sl], kh, _NT, preferred_element_type=F32)
        s = s + (cq_ref[:, h:h + 1] - cp_ref[h:h + 1, :]) * LOG2E
        m_ref[h], acc_ref[h] = _softmax_step(s, vh, m_ref[h], acc_ref[h])

    @pl.when(c == pl.num_programs(1) - 1)
    def _():
        rows = lax.broadcasted_iota(jnp.int32, (sq, sq), 0)
        cols = lax.broadcasted_iota(jnp.int32, (sq, sq), 1)
        for h in range(nh):
            sl = slice(h * HEAD_DIM, (h + 1) * HEAD_DIM)
            s = lax.dot_general(q_ref[:, sl], kn_ref[:, sl], _NT, preferred_element_type=F32)
            s = s + (cq_ref[:, h:h + 1] - cn_ref[h:h + 1, :]) * LOG2E
            s = jnp.where(cols <= rows, s, MASK_VALUE)
            _, acc = _softmax_step(s, vn_ref[:, sl], m_ref[h], acc_ref[h])
            o_ref[:, sl] = (acc[:, :HEAD_DIM] / acc[:, HEAD_DIM:]).astype(o_ref.dtype)


def _sample_attention(q, kn, vn, cache_k, cache_v, layer, c_new, c_past, tp=512):
    b, sq, width = q.shape
    nh = width // HEAD_DIM
    p = cache_k.shape[2]
    tp = min(tp, p)
    assert p % tp == 0
    nchunk = p // tp
    rows_k = cache_k.reshape(-1, HEAD_DIM)
    rows_v = cache_v.reshape(-1, HEAD_DIM)
    new_spec = pl.BlockSpec((None, sq, width), lambda bi, c: (bi, 0, 0))
    cache_spec = pl.BlockSpec((tp * nh, HEAD_DIM), lambda bi, c: ((layer * b + bi) * nchunk + c, 0))
    return pl.pallas_call(
        functools.partial(_sample_attn_kernel, nh=nh, tp=tp),
        out_shape=jax.ShapeDtypeStruct((b, sq, width), BF16),
        grid=(b, nchunk),
        in_specs=[new_spec, new_spec, new_spec, cache_spec, cache_spec,
                  pl.BlockSpec((None, sq, nh), lambda bi, c: (bi, 0, 0)),
                  pl.BlockSpec((None, nh, sq), lambda bi, c: (bi, 0, 0)),
                  pl.BlockSpec((None, nh, tp), lambda bi, c: (bi, 0, c))],
        out_specs=new_spec,
        scratch_shapes=[pltpu.VMEM((nh, sq, 1), F32), pltpu.VMEM((nh, sq, 2 * HEAD_DIM), F32)],
        compiler_params=_params(("parallel", "arbitrary"), 40),
        name="sample_attention",
    )(q, kn, vn, rows_k, rows_v, jnp.swapaxes(c_new, 1, 2), c_new, c_past)


def _outproj_kernel(p_ref, a_ref, wt_ref, wb_ref, r_ref, o_ref):
    acc = jnp.dot(p_ref[...], wt_ref[...].astype(BF16), preferred_element_type=F32)
    acc = acc + jnp.dot(a_ref[...], wb_ref[...].astype(BF16), preferred_element_type=F32)
    o_ref[...] = r_ref[...] + acc


def _out_proj(pooled, attn, w_out, resid, tn=512):
    m, half = pooled.shape
    assert attn.shape == (m, half) and w_out.shape[0] == 2 * half
    n = w_out.shape[1]
    tm = min(m, 1024)
    tn = min(tn, n)
    xspec = pl.BlockSpec((tm, half), lambda i, j: (i, 0))
    ospec = pl.BlockSpec((tm, tn), lambda i, j: (i, j))
    return pl.pallas_call(
        _outproj_kernel,
        out_shape=jax.ShapeDtypeStruct((m, n), F32),
        grid=(m // tm, n // tn),
        in_specs=[xspec, xspec,
                  pl.BlockSpec((half, tn), lambda i, j: (0, j)),
                  pl.BlockSpec((half, tn), lambda i, j: (1, j)),
                  ospec],
        out_specs=ospec,
        compiler_params=_params(("parallel", "arbitrary"), 56),
        name="out_proj",
    )(pooled, attn, w_out, w_out, resid)


def _down_kernel(x_ref, w_ref, r_ref, o_ref):
    kk = pl.program_id(2)
    d = jnp.dot(x_ref[...], w_ref[...].astype(BF16), preferred_element_type=F32)

    @pl.when(kk == 0)
    def _():
        o_ref[...] = r_ref[...] + d

    @pl.when(kk > 0)
    def _():
        o_ref[...] += d


def _down_proj(x, w, resid, tn=512, tk=4096):
    m, f = x.shape
    n = w.shape[1]
    tm = min(m, 1024)
    tn, tk = min(tn, n), min(tk, f)
    ospec = pl.BlockSpec((tm, tn), lambda i, j, kk: (i, j))
    return pl.pallas_call(
        _down_kernel,
        out_shape=jax.ShapeDtypeStruct((m, n), F32),
        grid=(m // tm, n // tn, f // tk),
        in_specs=[pl.BlockSpec((tm, tk), lambda i, j, kk: (i, kk)),
                  pl.BlockSpec((tk, tn), lambda i, j, kk: (kk, j)),
                  ospec],
        out_specs=ospec,
        compiler_params=_params(("parallel", "arbitrary", "arbitrary"), 56),
        name="down_proj",
    )(x, w, resid)


def _layer(x, pool_hist, past, wts, pos0):
    (attn_norm_g, w_in, w_f, b_f, q_norm_g, k_norm_g, w_pool, pool_scale, w_out, mlp_norm_g, w_up, w_down) = wts
    b, s, d = x.shape
    m = b * s
    pw = pool_scale.shape[0]
    aw = d - pw
    nh = aw // HEAD_DIM
    x2 = x.reshape(m, d)

    h = _rmsnorm(x2, attn_norm_g)
    (u,) = _proj(h, w_in, 0, pw, "plain")
    (qb,) = _proj(h, w_in, pw, aw, "qnorm", g=q_norm_g, scale=HEAD_DIM ** -0.5 * LOG2E)
    k, kb = _proj(h, w_in, pw + aw, aw, "knorm", g=k_norm_g)
    v, vb = _proj(h, w_in, pw + 2 * aw, aw, "both_t" if past is None else "both")
    logf = _forget_proj(h, w_f, b_f)

    u3 = u.reshape(b, s, pw)
    pooled = _pool_mixer(u3, pool_hist, w_pool, pool_scale, pos0).reshape(m, pw)
    new_hist = u3[:, s - POOL_HIST:]

    logf_t = jnp.swapaxes(logf.reshape(b, s, nh), 1, 2)
    if past is None:
        assert b == 1
        _, *pieces = _cumsum_rows(logf_t.reshape(nh, s), with_pieces=True)
        attn = _prompt_attention(qb, kb, vb, _bias_features(pieces, True), _bias_features(pieces, False))
    else:
        cache_k, cache_v, cache_logf, layer = past
        p = cache_k.shape[2]
        past_t = jnp.swapaxes(cache_logf[layer], 1, 2)
        (c_all,) = _cumsum_rows(jnp.concatenate([past_t, logf_t], axis=-1).reshape(b * nh, p + s), False)
        c_all = c_all.reshape(b, nh, p + s)
        attn = _sample_attention(qb.reshape(b, s, aw), kb.reshape(b, s, aw), vb.reshape(b, s, aw),
                                 cache_k, cache_v, layer, c_all[:, :, p:], c_all[:, :, :p]).reshape(m, aw)

    x1 = _out_proj(pooled, attn, w_out, x2)
    h2 = _rmsnorm(x1, mlp_norm_g)
    (hid,) = _proj(h2, w_up, 0, w_up.shape[1], "relu2")
    y = _down_proj(hid, w_down, x1)
    return (y.reshape(b, s, d), k.reshape(b, s, nh, HEAD_DIM), v.reshape(b, s, nh, HEAD_DIM),
            logf.reshape(b, s, nh), new_hist)


def kernel(x_prompt, x_sample, cache_k, cache_v, cache_logf, state_pool, attn_norm_g, w_in, b_f, q_norm_g,
           k_norm_g, w_pool, pool_scale, w_out, mlp_norm_g, w_up, w_down):
    depth = w_in.shape[0]
    pw = state_pool.shape[-1]
    nh = b_f.shape[-1]
    main_cols = w_in.shape[-1] - nh
    past_len = cache_k.shape[2]
    yp, ys = x_prompt, x_sample
    zero_hist = jnp.zeros((x_prompt.shape[0], POOL_HIST, pw), x_prompt.dtype)
    outs = [[] for _ in range(8)]
    for l in range(depth):
        w_f = jnp.pad(w_in[l][:, main_cols:].astype(BF16), ((0, 0), (0, HEAD_DIM - nh)))
        wts = (attn_norm_g[l], w_in[l], w_f, b_f[l], q_norm_g[l], k_norm_g[l], w_pool[l].astype(BF16),
               pool_scale[l], w_out[l], mlp_norm_g[l], w_up[l], w_down[l])
        yp, k1, v1, f1, h1 = _layer(yp, zero_hist, None, wts, 0)
        ys, k2, v2, f2, h2 = _layer(ys, state_pool[l], (cache_k, cache_v, cache_logf, l), wts, past_len)
        for acc, val in zip(outs, (k1, v1, f1, h1, k2, v2, f2, h2)):
            acc.append(val)
    return (yp, ys) + tuple(jnp.stack(o) for o in outs)
```

```python
import functools

import jax
import jax.numpy as jnp
from jax import lax
from jax.experimental import pallas as pl
from jax.experimental.pallas import tpu as pltpu

HEAD_DIM = 128
POOL_WINDOWS = (2, 4, 8, 16)
POOL_HIST = max(POOL_WINDOWS) - 1
HALO = POOL_HIST + 1
NORM_EPS = 1e-6
MASK_VALUE = -1e30
N_SPLIT = 3
LOG2E = 1.4426950408889634
ATTN_TQ = 2048
ATTN_TK = 1024
ATTN_QT = 256
ATTN_AHEAD = 3
MIB = 1024 * 1024

F32 = jnp.float32
BF16 = jnp.bfloat16


def _params(semantics, vmem_mib):
    return pltpu.CompilerParams(dimension_semantics=semantics, vmem_limit_bytes=vmem_mib * MIB)


def _rmsnorm_kernel(x_ref, g_ref, o_ref):
    x = x_ref[...]
    ms = jnp.mean(x * x, axis=-1, keepdims=True)
    o_ref[...] = (x * lax.rsqrt(ms + NORM_EPS) * g_ref[...]).astype(o_ref.dtype)


def _rmsnorm(x, g):
    m, d = x.shape
    tm = min(m, 256)
    return pl.pallas_call(
        _rmsnorm_kernel,
        out_shape=jax.ShapeDtypeStruct((m, d), BF16),
        grid=(m // tm,),
        in_specs=[pl.BlockSpec((tm, d), lambda i: (i, 0)), pl.BlockSpec((1, d), lambda i: (0, 0))],
        out_specs=pl.BlockSpec((tm, d), lambda i: (i, 0)),
        compiler_params=_params(("parallel",), 32),
        name="rmsnorm",
    )(x, g.reshape(1, d))


def _head_rmsnorm(a, g):
    ms = jnp.mean(a * a, axis=-1, keepdims=True)
    return a * lax.rsqrt(ms + NORM_EPS) * g


def _proj_kernel(x_ref, w_ref, *refs, mode, scale):
    acc = jnp.dot(x_ref[...], w_ref[...].astype(BF16), preferred_element_type=F32)
    tn = acc.shape[1]
    if mode == "plain":
        (o_ref,) = refs
        o_ref[...] = acc
    elif mode == "both":
        o_ref, ob_ref = refs
        o_ref[...] = acc
        ob_ref[...] = acc.astype(BF16)
    elif mode == "both_t":
        o_ref, ot_ref = refs
        o_ref[...] = acc
        tkv = ot_ref.shape[3]
        for h in range(ot_ref.shape[0]):
            for c in range(ot_ref.shape[1]):
                tile = acc[c * tkv:(c + 1) * tkv, h * HEAD_DIM:(h + 1) * HEAD_DIM]
                ot_ref[h, c] = tile.T.astype(BF16)
    elif mode == "relu2":
        (ob_ref,) = refs
        r = jnp.maximum(acc, 0.0)
        ob_ref[...] = (r * r).astype(BF16)
    elif mode == "qnorm":
        g_ref, ob_ref = refs
        for h in range(tn // HEAD_DIM):
            sl = slice(h * HEAD_DIM, (h + 1) * HEAD_DIM)
            ob_ref[:, sl] = (_head_rmsnorm(acc[:, sl], g_ref[...]) * scale).astype(BF16)
    elif mode == "knorm":
        g_ref, o_ref, ob_ref = refs
        for h in range(tn // HEAD_DIM):
            sl = slice(h * HEAD_DIM, (h + 1) * HEAD_DIM)
            y = _head_rmsnorm(acc[:, sl], g_ref[...])
            o_ref[:, sl] = y
            ob_ref[:, sl] = y.astype(BF16)
    else:
        raise ValueError(mode)


def _proj(x, w, col_off, ncols, mode, g=None, scale=1.0, tn=512, tkv=ATTN_TK):
    m, k = x.shape
    tm = min(m, 1024)
    tn = min(tn, ncols)
    tkv = min(tkv, tm)
    assert m % tm == 0 and ncols % tn == 0 and col_off % tn == 0 and tm % tkv == 0
    joff = col_off // tn
    in_specs = [pl.BlockSpec((tm, k), lambda i, j: (i, 0)),
                pl.BlockSpec((k, tn), lambda i, j: (0, j + joff))]
    args = [x, w]
    if g is not None:
        in_specs.append(pl.BlockSpec((1, HEAD_DIM), lambda i, j: (0, 0)))
        args.append(g.reshape(1, HEAD_DIM))
    ospec = pl.BlockSpec((tm, tn), lambda i, j: (i, j))
    f32_out = jax.ShapeDtypeStruct((m, ncols), F32)
    bf_out = jax.ShapeDtypeStruct((m, ncols), BF16)
    t_out = jax.ShapeDtypeStruct((ncols // HEAD_DIM, m // tkv, HEAD_DIM, tkv), BF16)
    t_spec = pl.BlockSpec((tn // HEAD_DIM, tm // tkv, HEAD_DIM, tkv), lambda i, j: (j, i, 0, 0))
    out_shape, out_specs = {
        "plain": ([f32_out], [ospec]),
        "both": ([f32_out, bf_out], [ospec, ospec]),
        "both_t": ([f32_out, t_out], [ospec, t_spec]),
        "relu2": ([bf_out], [ospec]),
        "qnorm": ([bf_out], [ospec]),
        "knorm": ([f32_out, bf_out], [ospec, ospec]),
    }[mode]
    return pl.pallas_call(
        functools.partial(_proj_kernel, mode=mode, scale=scale),
        out_shape=out_shape,
        grid=(m // tm, ncols // tn),
        in_specs=in_specs,
        out_specs=out_specs,
        compiler_params=_params(("parallel", "arbitrary"), 56),
        name="proj_" + mode,
    )(*args)


def _forget_kernel(x_ref, w_ref, b_ref, o_ref):
    nh = o_ref.shape[1]
    z = jnp.dot(x_ref[...], w_ref[...], preferred_element_type=F32)[:, :nh] + b_ref[...]
    o_ref[...] = jnp.minimum(z, 0.0) - jnp.log(1.0 + jnp.exp(-jnp.abs(z)))


def _forget_proj(x, w_f, b_f):
    m, k = x.shape
    nh = b_f.shape[0]
    tm = min(m, 1024)
    return pl.pallas_call(
        _forget_kernel,
        out_shape=jax.ShapeDtypeStruct((m, nh), F32),
        grid=(m // tm,),
        in_specs=[pl.BlockSpec((tm, k), lambda i: (i, 0)),
                  pl.BlockSpec((k, HEAD_DIM), lambda i: (0, 0)),
                  pl.BlockSpec((1, nh), lambda i: (0, 0))],
        out_specs=pl.BlockSpec((tm, nh), lambda i: (i, 0)),
        compiler_params=_params(("parallel",), 32),
        name="forget_proj",
    )(x, w_f, b_f.reshape(1, nh))


def _pool_kernel(u_ref, prev_ref, hist_ref, w_ref, scale_ref, o_ref, ext_ref, *, ts, pos0, group):
    i = pl.program_id(1)

    @pl.when(i == 0)
    def _():
        ext_ref[0:HALO, :] = hist_ref[...]

    @pl.when(i > 0)
    def _():
        ext_ref[0:HALO, :] = prev_ref[...]

    ext_ref[HALO:HALO + ts, :] = u_ref[...]
    pos = lax.broadcasted_iota(jnp.int32, (ts, 1), 0) + (i * ts + pos0)
    for gi, win in enumerate(POOL_WINDOWS):
        lo = gi * group
        cur = ext_ref[HALO:HALO + ts, lo:lo + group]
        tot = cur
        for j in range(1, win):
            tot = tot + ext_ref[HALO - j:HALO - j + ts, lo:lo + group]
        cnt = jnp.minimum(pos + 1, win).astype(F32)
        d = tot / cnt - cur
        out = jnp.dot(d.astype(BF16), w_ref[gi], preferred_element_type=F32)
        o_ref[:, lo:lo + group] = (out * scale_ref[:, lo:lo + group]).astype(BF16)


def _pool_mixer(u, hist, w_pool, pool_scale, pos0):
    b, s, c = u.shape
    group = c // len(POOL_WINDOWS)
    ts = min(s, 512)
    assert s % ts == 0 and ts % HALO == 0
    hist16 = jnp.pad(hist, ((0, 0), (HALO - POOL_HIST, 0), (0, 0)))
    nprev = ts // HALO
    return pl.pallas_call(
        functools.partial(_pool_kernel, ts=ts, pos0=pos0, group=group),
        out_shape=jax.ShapeDtypeStruct((b, s, c), BF16),
        grid=(b, s // ts),
        in_specs=[pl.BlockSpec((None, ts, c), lambda bi, i: (bi, i, 0)),
                  pl.BlockSpec((None, HALO, c), lambda bi, i: (bi, jnp.maximum(i * nprev - 1, 0), 0)),
                  pl.BlockSpec((None, HALO, c), lambda bi, i: (bi, 0, 0)),
                  pl.BlockSpec(w_pool.shape, lambda bi, i: (0, 0, 0)),
                  pl.BlockSpec((1, c), lambda bi, i: (0, 0))],
        out_specs=pl.BlockSpec((None, ts, c), lambda bi, i: (bi, i, 0)),
        scratch_shapes=[pltpu.VMEM((HALO + ts, c), F32)],
        compiler_params=_params(("parallel", "arbitrary"), 40),
        name="pool_mixer",
    )(u, u, hist16, w_pool, pool_scale.reshape(1, c))


def _cumsum_kernel(x_ref, c_ref, *piece_refs):
    x = x_ref[...]
    n = x.shape[1]
    lane = lax.broadcasted_iota(jnp.int32, x.shape, 1)
    shift = 1
    while shift < n:
        x = x + jnp.where(lane >= shift, pltpu.roll(x, shift, 1), 0.0)
        shift *= 2
    c_ref[...] = x
    rem = x * LOG2E
    for p_ref in piece_refs:
        piece = rem.astype(BF16)
        p_ref[...] = piece
        rem = rem - piece.astype(F32)


def _cumsum_rows(x, with_pieces):
    r, n = x.shape
    npad = -n % HEAD_DIM
    xp = jnp.pad(x, ((0, 0), (0, npad)))
    shp = xp.shape
    out_shape = [jax.ShapeDtypeStruct(shp, F32)]
    if with_pieces:
        out_shape += [jax.ShapeDtypeStruct(shp, BF16)] * N_SPLIT
    spec = pl.BlockSpec(shp, lambda: (0, 0))
    outs = pl.pallas_call(
        _cumsum_kernel,
        out_shape=out_shape,
        in_specs=[spec],
        out_specs=[spec] * len(out_shape),
        compiler_params=pltpu.CompilerParams(vmem_limit_bytes=32 * MIB),
        name="forget_cumsum",
    )(xp)
    return [o[:, :n] for o in outs]


_NT = (((1,), (1,)), ((), ()))


def _softmax_step(s, v, m_prev, acc_prev):
    m_new = jnp.maximum(m_prev, jnp.max(s, axis=1, keepdims=True))
    p = jnp.exp2(s - m_new).astype(BF16)
    v_ones = jnp.concatenate([v, jnp.ones(v.shape, BF16)], axis=1)
    pv = jnp.dot(p, v_ones, preferred_element_type=F32)
    return m_new, jnp.exp2(m_prev - m_new) * acc_prev + pv


def _attn_kernel(q_ref, qa_ref, k_ref, ka_ref, vt_ref, o_ref, m_ref, l_ref, acc_ref, *, tq, tk, qt):
    qi = pl.program_id(1)
    m_ref[...] = jnp.full(m_ref.shape, MASK_VALUE, F32)
    l_ref[...] = jnp.zeros(l_ref.shape, F32)
    acc_ref[...] = jnp.zeros(acc_ref.shape, F32)

    def step(j, diag):
        off = pl.multiple_of(j * tk, tk)
        kc = jnp.concatenate([k_ref[pl.ds(off, tk), :], ka_ref[pl.ds(off, tk), :]], axis=1)
        vt = vt_ref[j]
        groups = [t for t in range(tq // qt) if diag is None or (t + 1) * qt > diag * tk]

        def nkeys(t):
            return tk if diag is None else min((t + 1) * qt - diag * tk, tk)

        def scores(t):
            qs = slice(t * qt, (t + 1) * qt)
            nk = nkeys(t)
            qc = jnp.concatenate([q_ref[qs, :], qa_ref[qs, :]], axis=1)
            s = lax.dot_general(kc[:nk], qc, _NT, preferred_element_type=F32)
            if diag is not None and (t + 1) * qt <= (diag + 1) * tk:
                key = off + lax.broadcasted_iota(jnp.int32, (nk, 1), 0)
                qry = qi * tq + t * qt + lax.broadcasted_iota(jnp.int32, (1, qt), 1)
                s = jnp.where(key <= qry, s, MASK_VALUE)
            return s

        pending = [scores(t) for t in groups[:ATTN_AHEAD]]
        for gi, t in enumerate(groups):
            qs = slice(t * qt, (t + 1) * qt)
            s = pending.pop(0)
            if gi + ATTN_AHEAD < len(groups):
                pending.append(scores(groups[gi + ATTN_AHEAD]))
            m_prev = m_ref[:, qs]
            m_new = jnp.maximum(m_prev, jnp.max(s, axis=0, keepdims=True))
            p = jnp.exp2(s - m_new)
            alpha = jnp.exp2(m_prev - m_new)
            pv = jnp.dot(vt[:, :nkeys(t)], p.astype(BF16), preferred_element_type=F32)
            l_ref[:, qs] = alpha * l_ref[:, qs] + jnp.sum(p, axis=0, keepdims=True)
            acc_ref[:, qs] = alpha * acc_ref[:, qs] + pv
            m_ref[:, qs] = m_new

    per_q = tq // tk

    def body(j, carry):
        step(j, None)
        return carry

    lax.fori_loop(0, qi * per_q, body, 0)
    for d in range(per_q):
        step(qi * per_q + d, d)
    o_ref[...] = (acc_ref[...] / l_ref[...]).T.astype(o_ref.dtype)


def _prompt_attention(q, k, vt, qa, ka, tq=ATTN_TQ, qt=ATTN_QT):
    s, width = q.shape
    nh, nkv, _, tk = vt.shape
    tq = min(tq, s)
    qt = min(qt, tq)
    assert s % tq == 0 and tq % tk == 0 and tq % qt == 0 and nkv * tk == s
    row = lambda h, i: (i, h)
    full = lambda h, i: (0, h)
    return pl.pallas_call(
        functools.partial(_attn_kernel, tq=tq, tk=tk, qt=qt),
        out_shape=jax.ShapeDtypeStruct((s, width), BF16),
        grid=(nh, s // tq),
        in_specs=[pl.BlockSpec((tq, HEAD_DIM), row),
                  pl.BlockSpec((None, tq, HEAD_DIM), lambda h, i: (h, i, 0)),
                  pl.BlockSpec((s, HEAD_DIM), full),
                  pl.BlockSpec((None, s, HEAD_DIM), lambda h, i: (h, 0, 0)),
                  pl.BlockSpec((None, nkv, HEAD_DIM, tk), lambda h, i: (h, 0, 0, 0))],
        out_specs=pl.BlockSpec((tq, HEAD_DIM), row),
        scratch_shapes=[pltpu.VMEM((1, tq), F32), pltpu.VMEM((1, tq), F32), pltpu.VMEM((HEAD_DIM, tq), F32)],
        compiler_params=_params(("parallel", "arbitrary"), 48),
        name="prompt_attention",
    )(q, qa, k, ka, vt)


def _bias_features(pieces, query_side):
    c = jnp.stack(pieces, axis=-1)
    ones = jnp.ones_like(c)
    feats = jnp.concatenate([c, ones] if query_side else [ones, -c], axis=-1)
    return jnp.pad(feats, ((0, 0), (0, 0), (0, HEAD_DIM - 2 * N_SPLIT)))


def _sample_attn_kernel(q_ref, kn_ref, vn_ref, kc_ref, vc_ref, cq_ref, cn_ref, cp_ref, o_ref, m_ref, acc_ref,
                        *, nh, tp):
    c = pl.program_id(1)
    sq = q_ref.shape[0]

    @pl.when(c == 0)
    def _():
        m_ref[...] = jnp.full(m_ref.shape, MASK_VALUE, F32)
        acc_ref[...] = jnp.zeros(acc_ref.shape, F32)

    for h in range(nh):
        sl = slice(h * HEAD_DIM, (h + 1) * HEAD_DIM)
        kh = kc_ref[pl.ds(h, tp, stride=nh), :].astype(BF16)
        vh = vc_ref[pl.ds(h, tp, stride=nh), :].astype(BF16)
        s = lax.dot_general(q_ref[:, sl], kh, _NT, preferred_element_type=F32)
        s = s + (cq_ref[:, h:h + 1] - cp_ref[h:h + 1, :]) * LOG2E
        m_ref[h], acc_ref[h] = _softmax_step(s, vh, m_ref[h], acc_ref[h])

    @pl.when(c == pl.num_programs(1) - 1)
    def _():
        rows = lax.broadcasted_iota(jnp.int32, (sq, sq), 0)
        cols = lax.broadcasted_iota(jnp.int32, (sq, sq), 1)
        for h in range(nh):
            sl = slice(h * HEAD_DIM, (h + 1) * HEAD_DIM)
            s = lax.dot_general(q_ref[:, sl], kn_ref[:, sl], _NT, preferred_element_type=F32)
            s = s + (cq_ref[:, h:h + 1] - cn_ref[h:h + 1, :]) * LOG2E
            s = jnp.where(cols <= rows, s, MASK_VALUE)
            _, acc = _softmax_step(s, vn_ref[:, sl], m_ref[h], acc_ref[h])
            o_ref[:, sl] = (acc[:, :HEAD_DIM] / acc[:, HEAD_DIM:]).astype(o_ref.dtype)


def _sample_attention(q, kn, vn, cache_k, cache_v, layer, c_new, c_past, tp=512):
    b, sq, width = q.shape
    nh = width // HEAD_DIM
    p = cache_k.shape[2]
    tp = min(tp, p)
    assert p % tp == 0
    nchunk = p // tp
    rows_k = cache_k.reshape(-1, HEAD_DIM)
    rows_v = cache_v.reshape(-1, HEAD_DIM)
    new_spec = pl.BlockSpec((None, sq, width), lambda bi, c: (bi, 0, 0))
    cache_spec = pl.BlockSpec((tp * nh, HEAD_DIM), lambda bi, c: ((layer * b + bi) * nchunk + c, 0))
    return pl.pallas_call(
        functools.partial(_sample_attn_kernel, nh=nh, tp=tp),
        out_shape=jax.ShapeDtypeStruct((b, sq, width), BF16),
        grid=(b, nchunk),
        in_specs=[new_spec, new_spec, new_spec, cache_spec, cache_spec,
                  pl.BlockSpec((None, sq, nh), lambda bi, c: (bi, 0, 0)),
                  pl.BlockSpec((None, nh, sq), lambda bi, c: (bi, 0, 0)),
                  pl.BlockSpec((None, nh, tp), lambda bi, c: (bi, 0, c))],
        out_specs=new_spec,
        scratch_shapes=[pltpu.VMEM((nh, sq, 1), F32), pltpu.VMEM((nh, sq, 2 * HEAD_DIM), F32)],
        compiler_params=_params(("parallel", "arbitrary"), 40),
        name="sample_attention",
    )(q, kn, vn, rows_k, rows_v, jnp.swapaxes(c_new, 1, 2), c_new, c_past)


def _outproj_kernel(p_ref, a_ref, wt_ref, wb_ref, r_ref, o_ref):
    acc = jnp.dot(p_ref[...], wt_ref[...].astype(BF16), preferred_element_type=F32)
    acc = acc + jnp.dot(a_ref[...], wb_ref[...].astype(BF16), preferred_element_type=F32)
    o_ref[...] = r_ref[...] + acc


def _out_proj(pooled, attn, w_out, resid, tn=512):
    m, half = pooled.shape
    assert attn.shape == (m, half) and w_out.shape[0] == 2 * half
    n = w_out.shape[1]
    tm = min(m, 1024)
    tn = min(tn, n)
    xspec = pl.BlockSpec((tm, half), lambda i, j: (i, 0))
    ospec = pl.BlockSpec((tm, tn), lambda i, j: (i, j))
    return pl.pallas_call(
        _outproj_kernel,
        out_shape=jax.ShapeDtypeStruct((m, n), F32),
        grid=(m // tm, n // tn),
        in_specs=[xspec, xspec,
                  pl.BlockSpec((half, tn), lambda i, j: (0, j)),
                  pl.BlockSpec((half, tn), lambda i, j: (1, j)),
                  ospec],
        out_specs=ospec,
        compiler_params=_params(("parallel", "arbitrary"), 56),
        name="out_proj",
    )(pooled, attn, w_out, w_out, resid)


def _down_kernel(x_ref, w_ref, r_ref, o_ref):
    kk = pl.program_id(2)
    d = jnp.dot(x_ref[...], w_ref[...].astype(BF16), preferred_element_type=F32)

    @pl.when(kk == 0)
    def _():
        o_ref[...] = r_ref[...] + d

    @pl.when(kk > 0)
    def _():
        o_ref[...] += d


def _down_proj(x, w, resid, tn=512, tk=4096):
    m, f = x.shape
    n = w.shape[1]
    tm = min(m, 1024)
    tn, tk = min(tn, n), min(tk, f)
    ospec = pl.BlockSpec((tm, tn), lambda i, j, kk: (i, j))
    return pl.pallas_call(
        _down_kernel,
        out_shape=jax.ShapeDtypeStruct((m, n), F32),
        grid=(m // tm, n // tn, f // tk),
        in_specs=[pl.BlockSpec((tm, tk), lambda i, j, kk: (i, kk)),
                  pl.BlockSpec((tk, tn), lambda i, j, kk: (kk, j)),
                  ospec],
        out_specs=ospec,
        compiler_params=_params(("parallel", "arbitrary", "arbitrary"), 56),
        name="down_proj",
    )(x, w, resid)


def _layer(x, pool_hist, past, wts, pos0):
    (attn_norm_g, w_in, w_f, b_f, q_norm_g, k_norm_g, w_pool, pool_scale, w_out, mlp_norm_g, w_up, w_down) = wts
    b, s, d = x.shape
    m = b * s
    pw = pool_scale.shape[0]
    aw = d - pw
    nh = aw // HEAD_DIM
    x2 = x.reshape(m, d)

    h = _rmsnorm(x2, attn_norm_g)
    (u,) = _proj(h, w_in, 0, pw, "plain")
    (qb,) = _proj(h, w_in, pw, aw, "qnorm", g=q_norm_g, scale=HEAD_DIM ** -0.5 * LOG2E)
    k, kb = _proj(h, w_in, pw + aw, aw, "knorm", g=k_norm_g)
    v, vb = _proj(h, w_in, pw + 2 * aw, aw, "both_t" if past is None else "both")
    logf = _forget_proj(h, w_f, b_f)

    u3 = u.reshape(b, s, pw)
    pooled = _pool_mixer(u3, pool_hist, w_pool, pool_scale, pos0).reshape(m, pw)
    new_hist = u3[:, s - POOL_HIST:]

    logf_t = jnp.swapaxes(logf.reshape(b, s, nh), 1, 2)
    if past is None:
        assert b == 1
        _, *pieces = _cumsum_rows(logf_t.reshape(nh, s), with_pieces=True)
        attn = _prompt_attention(qb, kb, vb, _bias_features(pieces, True), _bias_features(pieces, False))
    else:
        cache_k, cache_v, cache_logf, layer = past
        p = cache_k.shape[2]
        past_t = jnp.swapaxes(cache_logf[layer], 1, 2)
        (c_all,) = _cumsum_rows(jnp.concatenate([past_t, logf_t], axis=-1).reshape(b * nh, p + s), False)
        c_all = c_all.reshape(b, nh, p + s)
        attn = _sample_attention(qb.reshape(b, s, aw), kb.reshape(b, s, aw), vb.reshape(b, s, aw),
                                 cache_k, cache_v, layer, c_all[:, :, p:], c_all[:, :, :p]).reshape(m, aw)

    x1 = _out_proj(pooled, attn, w_out, x2)
    h2 = _rmsnorm(x1, mlp_norm_g)
    (hid,) = _proj(h2, w_up, 0, w_up.shape[1], "relu2")
    y = _down_proj(hid, w_down, x1)
    return (y.reshape(b, s, d), k.reshape(b, s, nh, HEAD_DIM), v.reshape(b, s, nh, HEAD_DIM),
            logf.reshape(b, s, nh), new_hist)


def kernel(x_prompt, x_sample, cache_k, cache_v, cache_logf, state_pool, attn_norm_g, w_in, b_f, q_norm_g,
           k_norm_g, w_pool, pool_scale, w_out, mlp_norm_g, w_up, w_down):
    depth = w_in.shape[0]
    pw = state_pool.shape[-1]
    nh = b_f.shape[-1]
    main_cols = w_in.shape[-1] - nh
    past_len = cache_k.shape[2]
    yp, ys = x_prompt, x_sample
    zero_hist = jnp.zeros((x_prompt.shape[0], POOL_HIST, pw), x_prompt.dtype)
    outs = [[] for _ in range(8)]
    for l in range(depth):
        w_in_b = w_in[l].astype(BF16)
        w_f = jnp.pad(w_in_b[:, main_cols:], ((0, 0), (0, HEAD_DIM - nh)))
        wts = (attn_norm_g[l], w_in_b, w_f, b_f[l], q_norm_g[l], k_norm_g[l], w_pool[l].astype(BF16),
               pool_scale[l], w_out[l].astype(BF16), mlp_norm_g[l], w_up[l], w_down[l].astype(BF16))
        yp, k1, v1, f1, h1 = _layer(yp, zero_hist, None, wts, 0)
        ys, k2, v2, f2, h2 = _layer(ys, state_pool[l], (cache_k, cache_v, cache_logf, l), wts, past_len)
        for acc, val in zip(outs, (k1, v1, f1, h1, k2, v2, f2, h2)):
            acc.append(val)
    return (yp, ys) + tuple(jnp.stack(o) for o in outs)
```

```python
import functools

import jax
import jax.numpy as jnp
from jax import lax
from jax.experimental import pallas as pl
from jax.experimental.pallas import tpu as pltpu

HEAD_DIM = 128
POOL_WINDOWS = (2, 4, 8, 16)
POOL_HIST = max(POOL_WINDOWS) - 1
HALO = POOL_HIST + 1
NORM_EPS = 1e-6
MASK_VALUE = -1e30
N_SPLIT = 3
LOG2E = 1.4426950408889634
ATTN_TQ = 2048
ATTN_TK = 1024
ATTN_QT = 256
ONES_ROWS = 16
ATTN_AHEAD = 3
MIB = 1024 * 1024

F32 = jnp.float32
BF16 = jnp.bfloat16


def _params(semantics, vmem_mib):
    return pltpu.CompilerParams(dimension_semantics=semantics, vmem_limit_bytes=vmem_mib * MIB)


def _rmsnorm_kernel(x_ref, g_ref, o_ref):
    x = x_ref[...]
    ms = jnp.mean(x * x, axis=-1, keepdims=True)
    o_ref[...] = (x * lax.rsqrt(ms + NORM_EPS) * g_ref[...]).astype(o_ref.dtype)


def _rmsnorm(x, g):
    m, d = x.shape
    tm = min(m, 256)
    return pl.pallas_call(
        _rmsnorm_kernel,
        out_shape=jax.ShapeDtypeStruct((m, d), BF16),
        grid=(m // tm,),
        in_specs=[pl.BlockSpec((tm, d), lambda i: (i, 0)), pl.BlockSpec((1, d), lambda i: (0, 0))],
        out_specs=pl.BlockSpec((tm, d), lambda i: (i, 0)),
        compiler_params=_params(("parallel",), 32),
        name="rmsnorm",
    )(x, g.reshape(1, d))


def _head_rmsnorm(a, g):
    ms = jnp.mean(a * a, axis=-1, keepdims=True)
    return a * lax.rsqrt(ms + NORM_EPS) * g


def _proj_kernel(x_ref, w_ref, *refs, mode, scale):
    acc = jnp.dot(x_ref[...], w_ref[...].astype(BF16), preferred_element_type=F32)
    tn = acc.shape[1]
    if mode == "plain":
        (o_ref,) = refs
        o_ref[...] = acc
    elif mode == "both":
        o_ref, ob_ref = refs
        o_ref[...] = acc
        ob_ref[...] = acc.astype(BF16)
    elif mode == "both_t":
        o_ref, ot_ref = refs
        o_ref[...] = acc
        tkv = ot_ref.shape[3]
        for h in range(ot_ref.shape[0]):
            for c in range(ot_ref.shape[1]):
                tile = acc[c * tkv:(c + 1) * tkv, h * HEAD_DIM:(h + 1) * HEAD_DIM]
                ot_ref[h, c] = tile.T.astype(BF16)
    elif mode == "relu2":
        (ob_ref,) = refs
        r = jnp.maximum(acc, 0.0)
        ob_ref[...] = (r * r).astype(BF16)
    elif mode == "qnorm":
        g_ref, ob_ref = refs
        for h in range(tn // HEAD_DIM):
            sl = slice(h * HEAD_DIM, (h + 1) * HEAD_DIM)
            ob_ref[:, sl] = (_head_rmsnorm(acc[:, sl], g_ref[...]) * scale).astype(BF16)
    elif mode == "knorm":
        g_ref, o_ref, ob_ref = refs
        for h in range(tn // HEAD_DIM):
            sl = slice(h * HEAD_DIM, (h + 1) * HEAD_DIM)
            y = _head_rmsnorm(acc[:, sl], g_ref[...])
            o_ref[:, sl] = y
            ob_ref[:, sl] = y.astype(BF16)
    else:
        raise ValueError(mode)


def _proj(x, w, col_off, ncols, mode, g=None, scale=1.0, tn=512, tkv=ATTN_TK):
    m, k = x.shape
    tm = min(m, 1024)
    tn = min(tn, ncols)
    tkv = min(tkv, tm)
    assert m % tm == 0 and ncols % tn == 0 and col_off % tn == 0 and tm % tkv == 0
    joff = col_off // tn
    in_specs = [pl.BlockSpec((tm, k), lambda i, j: (i, 0)),
                pl.BlockSpec((k, tn), lambda i, j: (0, j + joff))]
    args = [x, w]
    if g is not None:
        in_specs.append(pl.BlockSpec((1, HEAD_DIM), lambda i, j: (0, 0)))
        args.append(g.reshape(1, HEAD_DIM))
    ospec = pl.BlockSpec((tm, tn), lambda i, j: (i, j))
    f32_out = jax.ShapeDtypeStruct((m, ncols), F32)
    bf_out = jax.ShapeDtypeStruct((m, ncols), BF16)
    t_out = jax.ShapeDtypeStruct((ncols // HEAD_DIM, m // tkv, HEAD_DIM, tkv), BF16)
    t_spec = pl.BlockSpec((tn // HEAD_DIM, tm // tkv, HEAD_DIM, tkv), lambda i, j: (j, i, 0, 0))
    out_shape, out_specs = {
        "plain": ([f32_out], [ospec]),
        "both": ([f32_out, bf_out], [ospec, ospec]),
        "both_t": ([f32_out, t_out], [ospec, t_spec]),
        "relu2": ([bf_out], [ospec]),
        "qnorm": ([bf_out], [ospec]),
        "knorm": ([f32_out, bf_out], [ospec, ospec]),
    }[mode]
    return pl.pallas_call(
        functools.partial(_proj_kernel, mode=mode, scale=scale),
        out_shape=out_shape,
        grid=(m // tm, ncols // tn),
        in_specs=in_specs,
        out_specs=out_specs,
        compiler_params=_params(("parallel", "arbitrary"), 56),
        name="proj_" + mode,
    )(*args)


def _forget_kernel(x_ref, w_ref, b_ref, o_ref):
    nh = o_ref.shape[1]
    z = jnp.dot(x_ref[...], w_ref[...], preferred_element_type=F32)[:, :nh] + b_ref[...]
    o_ref[...] = jnp.minimum(z, 0.0) - jnp.log(1.0 + jnp.exp(-jnp.abs(z)))


def _forget_proj(x, w_f, b_f):
    m, k = x.shape
    nh = b_f.shape[0]
    tm = min(m, 1024)
    return pl.pallas_call(
        _forget_kernel,
        out_shape=jax.ShapeDtypeStruct((m, nh), F32),
        grid=(m // tm,),
        in_specs=[pl.BlockSpec((tm, k), lambda i: (i, 0)),
                  pl.BlockSpec((k, HEAD_DIM), lambda i: (0, 0)),
                  pl.BlockSpec((1, nh), lambda i: (0, 0))],
        out_specs=pl.BlockSpec((tm, nh), lambda i: (i, 0)),
        compiler_params=_params(("parallel",), 32),
        name="forget_proj",
    )(x, w_f, b_f.reshape(1, nh))


def _pool_kernel(u_ref, prev_ref, hist_ref, w_ref, scale_ref, o_ref, ext_ref, *, ts, pos0, group):
    i = pl.program_id(1)

    @pl.when(i == 0)
    def _():
        ext_ref[0:HALO, :] = hist_ref[...]

    @pl.when(i > 0)
    def _():
        ext_ref[0:HALO, :] = prev_ref[...]

    ext_ref[HALO:HALO + ts, :] = u_ref[...]
    pos = lax.broadcasted_iota(jnp.int32, (ts, 1), 0) + (i * ts + pos0)
    for gi, win in enumerate(POOL_WINDOWS):
        lo = gi * group
        cur = ext_ref[HALO:HALO + ts, lo:lo + group]
        tot = cur
        for j in range(1, win):
            tot = tot + ext_ref[HALO - j:HALO - j + ts, lo:lo + group]
        cnt = jnp.minimum(pos + 1, win).astype(F32)
        d = tot / cnt - cur
        out = jnp.dot(d.astype(BF16), w_ref[gi], preferred_element_type=F32)
        o_ref[:, lo:lo + group] = (out * scale_ref[:, lo:lo + group]).astype(BF16)


def _pool_mixer(u, hist, w_pool, pool_scale, pos0):
    b, s, c = u.shape
    group = c // len(POOL_WINDOWS)
    ts = min(s, 512)
    assert s % ts == 0 and ts % HALO == 0
    hist16 = jnp.pad(hist, ((0, 0), (HALO - POOL_HIST, 0), (0, 0)))
    nprev = ts // HALO
    return pl.pallas_call(
        functools.partial(_pool_kernel, ts=ts, pos0=pos0, group=group),
        out_shape=jax.ShapeDtypeStruct((b, s, c), BF16),
        grid=(b, s // ts),
        in_specs=[pl.BlockSpec((None, ts, c), lambda bi, i: (bi, i, 0)),
                  pl.BlockSpec((None, HALO, c), lambda bi, i: (bi, jnp.maximum(i * nprev - 1, 0), 0)),
                  pl.BlockSpec((None, HALO, c), lambda bi, i: (bi, 0, 0)),
                  pl.BlockSpec(w_pool.shape, lambda bi, i: (0, 0, 0)),
                  pl.BlockSpec((1, c), lambda bi, i: (0, 0))],
        out_specs=pl.BlockSpec((None, ts, c), lambda bi, i: (bi, i, 0)),
        scratch_shapes=[pltpu.VMEM((HALO + ts, c), F32)],
        compiler_params=_params(("parallel", "arbitrary"), 40),
        name="pool_mixer",
    )(u, u, hist16, w_pool, pool_scale.reshape(1, c))


def _cumsum_kernel(x_ref, c_ref, *piece_refs):
    x = x_ref[...]
    n = x.shape[1]
    lane = lax.broadcasted_iota(jnp.int32, x.shape, 1)
    shift = 1
    while shift < n:
        x = x + jnp.where(lane >= shift, pltpu.roll(x, shift, 1), 0.0)
        shift *= 2
    c_ref[...] = x
    rem = x * LOG2E
    for p_ref in piece_refs:
        piece = rem.astype(BF16)
        p_ref[...] = piece
        rem = rem - piece.astype(F32)


def _cumsum_rows(x, with_pieces):
    r, n = x.shape
    npad = -n % HEAD_DIM
    xp = jnp.pad(x, ((0, 0), (0, npad)))
    shp = xp.shape
    out_shape = [jax.ShapeDtypeStruct(shp, F32)]
    if with_pieces:
        out_shape += [jax.ShapeDtypeStruct(shp, BF16)] * N_SPLIT
    spec = pl.BlockSpec(shp, lambda: (0, 0))
    outs = pl.pallas_call(
        _cumsum_kernel,
        out_shape=out_shape,
        in_specs=[spec],
        out_specs=[spec] * len(out_shape),
        compiler_params=pltpu.CompilerParams(vmem_limit_bytes=32 * MIB),
        name="forget_cumsum",
    )(xp)
    return [o[:, :n] for o in outs]


_NT = (((1,), (1,)), ((), ()))


def _softmax_step(s, v, m_prev, acc_prev):
    m_new = jnp.maximum(m_prev, jnp.max(s, axis=1, keepdims=True))
    p = jnp.exp2(s - m_new).astype(BF16)
    v_ones = jnp.concatenate([v, jnp.ones(v.shape, BF16)], axis=1)
    pv = jnp.dot(p, v_ones, preferred_element_type=F32)
    return m_new, jnp.exp2(m_prev - m_new) * acc_prev + pv


def _attn_kernel(q_ref, qa_ref, k_ref, ka_ref, vt_ref, o_ref, m_ref, acc_ref, *, tq, tk, qt):
    qi = pl.program_id(1)
    m_ref[...] = jnp.full(m_ref.shape, MASK_VALUE, F32)
    acc_ref[...] = jnp.zeros(acc_ref.shape, F32)

    def step(j, diag):
        off = pl.multiple_of(j * tk, tk)
        kc = jnp.concatenate([k_ref[pl.ds(off, tk), :], ka_ref[pl.ds(off, tk), :]], axis=1)
        vt = jnp.concatenate([vt_ref[j], jnp.ones((ONES_ROWS, tk), BF16)], axis=0)
        groups = [t for t in range(tq // qt) if diag is None or (t + 1) * qt > diag * tk]

        def nkeys(t):
            return tk if diag is None else min((t + 1) * qt - diag * tk, tk)

        def scores(t):
            qs = slice(t * qt, (t + 1) * qt)
            nk = nkeys(t)
            qc = jnp.concatenate([q_ref[qs, :], qa_ref[qs, :]], axis=1)
            s = lax.dot_general(kc[:nk], qc, _NT, preferred_element_type=F32)
            if diag is not None and (t + 1) * qt <= (diag + 1) * tk:
                key = off + lax.broadcasted_iota(jnp.int32, (nk, 1), 0)
                qry = qi * tq + t * qt + lax.broadcasted_iota(jnp.int32, (1, qt), 1)
                s = jnp.where(key <= qry, s, MASK_VALUE)
            return s

        pending = [scores(t) for t in groups[:ATTN_AHEAD]]
        for gi, t in enumerate(groups):
            qs = slice(t * qt, (t + 1) * qt)
            s = pending.pop(0)
            if gi + ATTN_AHEAD < len(groups):
                pending.append(scores(groups[gi + ATTN_AHEAD]))
            m_prev = m_ref[:, qs]
            m_new = jnp.maximum(m_prev, jnp.max(s, axis=0, keepdims=True))
            p = jnp.exp2(s - m_new).astype(BF16)
            pv = jnp.dot(vt[:, :nkeys(t)], p, preferred_element_type=F32)
            acc_ref[:, qs] = jnp.exp2(m_prev - m_new) * acc_ref[:, qs] + pv
            m_ref[:, qs] = m_new

    per_q = tq // tk

    def body(j, carry):
        step(j, None)
        return carry

    lax.fori_loop(0, qi * per_q, body, 0)
    for d in range(per_q):
        step(qi * per_q + d, d)
    o_ref[...] = (acc_ref[:HEAD_DIM, :] / acc_ref[HEAD_DIM:HEAD_DIM + 1, :]).T.astype(o_ref.dtype)


def _prompt_attention(q, k, vt, qa, ka, tq=ATTN_TQ, qt=ATTN_QT):
    s, width = q.shape
    nh, nkv, _, tk = vt.shape
    tq = min(tq, s)
    qt = min(qt, tq)
    assert s % tq == 0 and tq % tk == 0 and tq % qt == 0 and nkv * tk == s
    row = lambda h, i: (i, h)
    full = lambda h, i: (0, h)
    return pl.pallas_call(
        functools.partial(_attn_kernel, tq=tq, tk=tk, qt=qt),
        out_shape=jax.ShapeDtypeStruct((s, width), BF16),
        grid=(nh, s // tq),
        in_specs=[pl.BlockSpec((tq, HEAD_DIM), row),
                  pl.BlockSpec((None, tq, HEAD_DIM), lambda h, i: (h, i, 0)),
                  pl.BlockSpec((s, HEAD_DIM), full),
                  pl.BlockSpec((None, s, HEAD_DIM), lambda h, i: (h, 0, 0)),
                  pl.BlockSpec((None, nkv, HEAD_DIM, tk), lambda h, i: (h, 0, 0, 0))],
        out_specs=pl.BlockSpec((tq, HEAD_DIM), row),
        scratch_shapes=[pltpu.VMEM((1, tq), F32), pltpu.VMEM((HEAD_DIM + ONES_ROWS, tq), F32)],
        compiler_params=_params(("parallel", "arbitrary"), 48),
        name="prompt_attention",
    )(q, qa, k, ka, vt)


def _bias_features(pieces, query_side):
    c = jnp.stack(pieces, axis=-1)
    ones = jnp.ones_like(c)
    feats = jnp.concatenate([c, ones] if query_side else [ones, -c], axis=-1)
    return jnp.pad(feats, ((0, 0), (0, 0), (0, HEAD_DIM - 2 * N_SPLIT)))


def _sample_attn_kernel(q_ref, kn_ref, vn_ref, kc_ref, vc_ref, cq_ref, cn_ref, cp_ref, o_ref, m_ref, acc_ref,
                        *, nh, tp):
    c = pl.program_id(1)
    sq = q_ref.shape[0]

    @pl.when(c == 0)
    def _():
        m_ref[...] = jnp.full(m_ref.shape, MASK_VALUE, F32)
        acc_ref[...] = jnp.zeros(acc_ref.shape, F32)

    for h in range(nh):
        sl = slice(h * HEAD_DIM, (h + 1) * HEAD_DIM)
        kh = kc_ref[pl.ds(h, tp, stride=nh), :].astype(BF16)
        vh = vc_ref[pl.ds(h, tp, stride=nh), :].astype(BF16)
        s = lax.dot_general(q_ref[:, sl], kh, _NT, preferred_element_type=F32)
        s = s + (cq_ref[:, h:h + 1] - cp_ref[h:h + 1, :]) * LOG2E
        m_ref[h], acc_ref[h] = _softmax_step(s, vh, m_ref[h], acc_ref[h])

    @pl.when(c == pl.num_programs(1) - 1)
    def _():
        rows = lax.broadcasted_iota(jnp.int32, (sq, sq), 0)
        cols = lax.broadcasted_iota(jnp.int32, (sq, sq), 1)
        for h in range(nh):
            sl = slice(h * HEAD_DIM, (h + 1) * HEAD_DIM)
            s = lax.dot_general(q_ref[:, sl], kn_ref[:, sl], _NT, preferred_element_type=F32)
            s = s + (cq_ref[:, h:h + 1] - cn_ref[h:h + 1, :]) * LOG2E
            s = jnp.where(cols <= rows, s, MASK_VALUE)
            _, acc = _softmax_step(s, vn_ref[:, sl], m_ref[h], acc_ref[h])
            o_ref[:, sl] = (acc[:, :HEAD_DIM] / acc[:, HEAD_DIM:]).astype(o_ref.dtype)


def _sample_attention(q, kn, vn, cache_k, cache_v, layer, c_new, c_past, tp=512):
    b, sq, width = q.shape
    nh = width // HEAD_DIM
    p = cache_k.shape[2]
    tp = min(tp, p)
    assert p % tp == 0
    nchunk = p // tp
    rows_k = cache_k.reshape(-1, HEAD_DIM)
    rows_v = cache_v.reshape(-1, HEAD_DIM)
    new_spec = pl.BlockSpec((None, sq, width), lambda bi, c: (bi, 0, 0))
    cache_spec = pl.BlockSpec((tp * nh, HEAD_DIM), lambda bi, c: ((layer * b + bi) * nchunk + c, 0))
    return pl.pallas_call(
        functools.partial(_sample_attn_kernel, nh=nh, tp=tp),
        out_shape=jax.ShapeDtypeStruct((b, sq, width), BF16),
        grid=(b, nchunk),
        in_specs=[new_spec, new_spec, new_spec, cache_spec, cache_spec,
                  pl.BlockSpec((None, sq, nh), lambda bi, c: (bi, 0, 0)),
                  pl.BlockSpec((None, nh, sq), lambda bi, c: (bi, 0, 0)),
                  pl.BlockSpec((None, nh, tp), lambda bi, c: (bi, 0, c))],
        out_specs=new_spec,
        scratch_shapes=[pltpu.VMEM((nh, sq, 1), F32), pltpu.VMEM((nh, sq, 2 * HEAD_DIM), F32)],
        compiler_params=_params(("parallel", "arbitrary"), 40),
        name="sample_attention",
    )(q, kn, vn, rows_k, rows_v, jnp.swapaxes(c_new, 1, 2), c_new, c_past)


def _outproj_kernel(p_ref, a_ref, wt_ref, wb_ref, r_ref, o_ref):
    acc = jnp.dot(p_ref[...], wt_ref[...].astype(BF16), preferred_element_type=F32)
    acc = acc + jnp.dot(a_ref[...], wb_ref[...].astype(BF16), preferred_element_type=F32)
    o_ref[...] = r_ref[...] + acc


def _out_proj(pooled, attn, w_out, resid, tn=512):
    m, half = pooled.shape
    assert attn.shape == (m, half) and w_out.shape[0] == 2 * half
    n = w_out.shape[1]
    tm = min(m, 1024)
    tn = min(tn, n)
    xspec = pl.BlockSpec((tm, half), lambda i, j: (i, 0))
    ospec = pl.BlockSpec((tm, tn), lambda i, j: (i, j))
    return pl.pallas_call(
        _outproj_kernel,
        out_shape=jax.ShapeDtypeStruct((m, n), F32),
        grid=(m // tm, n // tn),
        in_specs=[xspec, xspec,
                  pl.BlockSpec((half, tn), lambda i, j: (0, j)),
                  pl.BlockSpec((half, tn), lambda i, j: (1, j)),
                  ospec],
        out_specs=ospec,
        compiler_params=_params(("parallel", "arbitrary"), 56),
        name="out_proj",
    )(pooled, attn, w_out, w_out, resid)


def _down_kernel(x_ref, w_ref, r_ref, o_ref):
    kk = pl.program_id(2)
    d = jnp.dot(x_ref[...], w_ref[...].astype(BF16), preferred_element_type=F32)

    @pl.when(kk == 0)
    def _():
        o_ref[...] = r_ref[...] + d

    @pl.when(kk > 0)
    def _():
        o_ref[...] += d


def _down_proj(x, w, resid, tn=512, tk=4096):
    m, f = x.shape
    n = w.shape[1]
    tm = min(m, 1024)
    tn, tk = min(tn, n), min(tk, f)
    ospec = pl.BlockSpec((tm, tn), lambda i, j, kk: (i, j))
    return pl.pallas_call(
        _down_kernel,
        out_shape=jax.ShapeDtypeStruct((m, n), F32),
        grid=(m // tm, n // tn, f // tk),
        in_specs=[pl.BlockSpec((tm, tk), lambda i, j, kk: (i, kk)),
                  pl.BlockSpec((tk, tn), lambda i, j, kk: (kk, j)),
                  ospec],
        out_specs=ospec,
        compiler_params=_params(("parallel", "arbitrary", "arbitrary"), 56),
        name="down_proj",
    )(x, w, resid)


def _layer(x, pool_hist, past, wts, pos0):
    (attn_norm_g, w_in, w_f, b_f, q_norm_g, k_norm_g, w_pool, pool_scale, w_out, mlp_norm_g, w_up, w_down) = wts
    b, s, d = x.shape
    m = b * s
    pw = pool_scale.shape[0]
    aw = d - pw
    nh = aw // HEAD_DIM
    x2 = x.reshape(m, d)

    h = _rmsnorm(x2, attn_norm_g)
    (u,) = _proj(h, w_in, 0, pw, "plain")
    (qb,) = _proj(h, w_in, pw, aw, "qnorm", g=q_norm_g, scale=HEAD_DIM ** -0.5 * LOG2E)
    k, kb = _proj(h, w_in, pw + aw, aw, "knorm", g=k_norm_g)
    v, vb = _proj(h, w_in, pw + 2 * aw, aw, "both_t" if past is None else "both")
    logf = _forget_proj(h, w_f, b_f)

    u3 = u.reshape(b, s, pw)
    pooled = _pool_mixer(u3, pool_hist, w_pool, pool_scale, pos0).reshape(m, pw)
    new_hist = u3[:, s - POOL_HIST:]

    logf_t = jnp.swapaxes(logf.reshape(b, s, nh), 1, 2)
    if past is None:
        assert b == 1
        _, *pieces = _cumsum_rows(logf_t.reshape(nh, s), with_pieces=True)
        attn = _prompt_attention(qb, kb, vb, _bias_features(pieces, True), _bias_features(pieces, False))
    else:
        cache_k, cache_v, cache_logf, layer = past
        p = cache_k.shape[2]
        past_t = jnp.swapaxes(cache_logf[layer], 1, 2)
        (c_all,) = _cumsum_rows(jnp.concatenate([past_t, logf_t], axis=-1).reshape(b * nh, p + s), False)
        c_all = c_all.reshape(b, nh, p + s)
        attn = _sample_attention(qb.reshape(b, s, aw), kb.reshape(b, s, aw), vb.reshape(b, s, aw),
                                 cache_k, cache_v, layer, c_all[:, :, p:], c_all[:, :, :p]).reshape(m, aw)

    x1 = _out_proj(pooled, attn, w_out, x2)
    h2 = _rmsnorm(x1, mlp_norm_g)
    (hid,) = _proj(h2, w_up, 0, w_up.shape[1], "relu2")
    y = _down_proj(hid, w_down, x1)
    return (y.reshape(b, s, d), k.reshape(b, s, nh, HEAD_DIM), v.reshape(b, s, nh, HEAD_DIM),
            logf.reshape(b, s, nh), new_hist)


def kernel(x_prompt, x_sample, cache_k, cache_v, cache_logf, state_pool, attn_norm_g, w_in, b_f, q_norm_g,
           k_norm_g, w_pool, pool_scale, w_out, mlp_norm_g, w_up, w_down):
    depth = w_in.shape[0]
    pw = state_pool.shape[-1]
    nh = b_f.shape[-1]
    main_cols = w_in.shape[-1] - nh
    past_len = cache_k.shape[2]
    yp, ys = x_prompt, x_sample
    zero_hist = jnp.zeros((x_prompt.shape[0], POOL_HIST, pw), x_prompt.dtype)
    outs = [[] for _ in range(8)]
    for l in range(depth):
        w_in_b = w_in[l].astype(BF16)
        w_f = jnp.pad(w_in_b[:, main_cols:], ((0, 0), (0, HEAD_DIM - nh)))
        wts = (attn_norm_g[l], w_in_b, w_f, b_f[l], q_norm_g[l], k_norm_g[l], w_pool[l].astype(BF16),
               pool_scale[l], w_out[l].astype(BF16), mlp_norm_g[l], w_up[l], w_down[l].astype(BF16))
        yp, k1, v1, f1, h1 = _layer(yp, zero_hist, None, wts, 0)
        ys, k2, v2, f2, h2 = _layer(ys, state_pool[l], (cache_k, cache_v, cache_logf, l), wts, past_len)
        for acc, val in zip(outs, (k1, v1, f1, h1, k2, v2, f2, h2)):
            acc.append(val)
    return (yp, ys) + tuple(jnp.stack(o) for o in outs)
```
